```python
import jax, jax.numpy as jnp
from jax import lax
import numpy as np

D_MODEL = 1024
BATCH = 8
SEQ = 4096
DEPTH = 2

GRID_W = 64
D_MIX = D_MODEL
D_MLSTM = D_MIX // 2
D_NA = D_MIX - D_MLSTM
MLSTM_HEADS = 4
MLSTM_HD = D_MLSTM // MLSTM_HEADS
NA_HEADS = 8
NA_HD = D_NA // NA_HEADS
CHUNK = 64
CONV_K = 3
NA_KH_MAX = 8
NA_KW = 16
D_FF = 4 * D_MODEL
N_GATES = 4 * MLSTM_HEADS
D_IN = 4 * D_MLSTM + N_GATES + 3 * D_NA
EPS = 1e-6

kernel_name = "hybrid_mlstm_natten_encoder"


def rms_norm(x, w):
    xf = x.astype(jnp.float32)
    y = xf * lax.rsqrt(jnp.mean(xf * xf, axis=-1, keepdims=True) + EPS)
    return (y * w.astype(jnp.float32)).astype(x.dtype)


def centered_dwconv(x, w, b):
    c = x.shape[-1]
    y = lax.conv_general_dilated(
        x, w[:, None, :].astype(x.dtype), window_strides=(1,),
        padding=[(CONV_K // 2, CONV_K // 2)],
        dimension_numbers=("NWC", "WIO", "NWC"), feature_group_count=c)
    return y + b.astype(x.dtype)


def mlstm_chunkwise(q, k, v, i_pre, f_pre):
    bsz, nh, s, dh = q.shape
    nc = s // CHUNK
    q = q.reshape(bsz, nh, nc, CHUNK, dh)
    k = k.reshape(bsz, nh, nc, CHUNK, dh) * (dh ** -0.5)
    v = v.reshape(bsz, nh, nc, CHUNK, dh)
    logf = jax.nn.log_sigmoid(f_pre).reshape(bsz, nh, nc, CHUNK)
    ig = i_pre.reshape(bsz, nh, nc, CHUNK)
    b = jnp.cumsum(logf, axis=-1)
    total = b[..., -1]

    a = total[..., None] - b + ig
    m_loc = jnp.max(a, axis=-1)
    wa = jnp.exp(a - m_loc[..., None])
    c_loc = jnp.einsum("bhnlv,bhnlk->bhnvk", wa[..., None] * v, k)
    n_loc = jnp.einsum("bhnl,bhnlk->bhnk", wa, k)

    def step(carry, inp):
        c_st, n_st, m_st = carry
        tot, ml, cl, nl = inp
        m_new = jnp.maximum(tot + m_st, ml)
        s_old = jnp.exp(tot + m_st - m_new)
        s_loc = jnp.exp(ml - m_new)
        c_new = s_old[..., None, None] * c_st + s_loc[..., None, None] * cl
        n_new = s_old[..., None] * n_st + s_loc[..., None] * nl
        return (c_new, n_new, m_new), (c_st, n_st, m_st)

    init = (jnp.zeros((bsz, nh, dh, dh), jnp.float32),
            jnp.zeros((bsz, nh, dh), jnp.float32),
            jnp.zeros((bsz, nh), jnp.float32))
    xs = (jnp.moveaxis(total, 2, 0), jnp.moveaxis(m_loc, 2, 0),
          jnp.moveaxis(c_loc, 2, 0), jnp.moveaxis(n_loc, 2, 0))
    _, (c_prev, n_prev, m_prev) = lax.scan(step, init, xs)
    c_prev = jnp.moveaxis(c_prev, 0, 2)
    n_prev = jnp.moveaxis(n_prev, 0, 2)
    m_prev = jnp.moveaxis(m_prev, 0, 2)

    tri = jnp.tril(jnp.ones((CHUNK, CHUNK), dtype=bool))
    dmat = b[..., :, None] - b[..., None, :] + ig[..., None, :]
    dmat = jnp.where(tri, dmat, -jnp.inf)
    inter = b + m_prev[..., None]
    m = jnp.maximum(inter, jnp.max(dmat, axis=-1))
    w_inter = jnp.exp(inter - m)
    p = jnp.exp(dmat - m[..., None]) * jnp.einsum("bhnjd,bhnsd->bhnjs", q, k)
    num = (w_inter[..., None] * jnp.einsum("bhnvk,bhnjk->bhnjv", c_prev, q)
           + jnp.einsum("bhnjs,bhnsv->bhnjv", p, v))
    den = w_inter * jnp.einsum("bhnk,bhnjk->bhnj", n_prev, q) + jnp.sum(p, axis=-1)
    h = num / jnp.maximum(jnp.abs(den), jnp.exp(-m))[..., None]
    return h.reshape(bsz, nh, s, dh)


def mlstm_mixer(q, k, v, o_pre, gates, gate_b, norm_w):
    bsz, s, _ = v.shape
    f32 = jnp.float32

    def heads(t):
        return t.astype(f32).reshape(bsz, s, MLSTM_HEADS, MLSTM_HD).transpose(0, 2, 1, 3)

    qh, kh, vh = heads(q), heads(k), heads(v)
    g = (gates.astype(f32) + gate_b.astype(f32)).reshape(bsz, s, 4, MLSTM_HEADS)
    g = g.transpose(2, 0, 3, 1)
    h_fwd = mlstm_chunkwise(qh, kh, vh, g[0], g[1])

    def flip(t):
        return jnp.flip(t, axis=2)

    h_bwd = flip(mlstm_chunkwise(flip(qh), flip(kh), flip(vh), flip(g[2]), flip(g[3])))
    h = h_fwd + h_bwd
    h = h * lax.rsqrt(jnp.mean(h * h, axis=-1, keepdims=True) + EPS)
    h = h.transpose(0, 2, 1, 3).reshape(bsz, s, D_MLSTM) * norm_w.astype(f32)
    return (jax.nn.sigmoid(o_pre.astype(f32)) * h).astype(v.dtype)


def neighbourhood_attention(q, k, v, rpb):
    bsz, s, _ = q.shape
    rows = s // GRID_W
    kh = min(NA_KH_MAX, rows)

    def grid(t):
        return t.reshape(bsz, rows, GRID_W, NA_HEADS, NA_HD)

    qg = grid(q) * (NA_HD ** -0.5)
    kg, vg = grid(k), grid(v)
    cols = np.arange(GRID_W)
    c_start = np.clip(cols - NA_KW // 2, 0, GRID_W - NA_KW)
    col_idx = c_start[:, None] + np.arange(NA_KW)[None, :]
    rel_c = col_idx - cols[:, None] + NA_KW - 1
    rpb_c = rpb[:, :, rel_c]

    def row_block(args):
        r, q_row = args
        rs = jnp.clip(r - kh // 2, 0, rows - kh)
        k_band = lax.dynamic_slice_in_dim(kg, rs, kh, axis=1)
        v_band = lax.dynamic_slice_in_dim(vg, rs, kh, axis=1)
        k_win = k_band[:, :, col_idx]
        v_win = v_band[:, :, col_idx]
        rel_r = rs + jnp.arange(kh) - r + NA_KH_MAX - 1
        bias = rpb_c[:, rel_r].transpose(0, 2, 1, 3)
        sc = jnp.einsum("bchd,brcwhd->bhcrw", q_row, k_win).astype(jnp.float32)
        sc = sc + bias[None].astype(jnp.float32)
        p = jax.nn.softmax(sc.reshape(bsz, NA_HEADS, GRID_W, kh * NA_KW), axis=-1)
        p = p.reshape(bsz, NA_HEADS, GRID_W, kh, NA_KW).astype(v.dtype)
        return jnp.einsum("bhcrw,brcwhd->bchd", p, v_win)

    out = lax.map(row_block, (jnp.arange(rows), jnp.moveaxis(qg, 1, 0)))
    return jnp.moveaxis(out, 0, 1).reshape(bsz, s, D_NA)


def setup_inputs(seed: int = 0) -> dict:
    key = jax.random.key(seed)
    ks = jax.random.split(key, 16)

    def nrm(k, shape, scale):
        return jax.random.normal(k, shape, jnp.float32) * scale

    x = nrm(ks[0], (BATCH, SEQ, D_MODEL), 1.0)
    norm1_w = 1.0 + nrm(ks[1], (DEPTH, D_MODEL), 0.02)
    w_in = nrm(ks[2], (DEPTH, D_MODEL, D_IN), D_MODEL ** -0.5)
    conv_w = nrm(ks[3], (DEPTH, CONV_K, 2 * D_MLSTM), CONV_K ** -0.5)
    conv_b = nrm(ks[4], (DEPTH, 2 * D_MLSTM), 0.02)
    f_bias = jnp.linspace(3.0, 6.0, MLSTM_HEADS, dtype=jnp.float32)
    i_bias = jnp.zeros((MLSTM_HEADS,), jnp.float32)
    base = jnp.concatenate([i_bias, f_bias, i_bias, f_bias])
    gate_b = base[None, :] + nrm(ks[5], (DEPTH, N_GATES), 0.1)
    mlstm_norm_w = 1.0 + nrm(ks[6], (DEPTH, D_MLSTM), 0.02)
    rpb = nrm(ks[7], (DEPTH, NA_HEADS, 2 * NA_KH_MAX - 1, 2 * NA_KW - 1), 0.1)
    w_out = nrm(ks[8], (DEPTH, D_MIX, D_MODEL), D_MIX ** -0.5)
    norm2_w = 1.0 + nrm(ks[9], (DEPTH, D_MODEL), 0.02)
    w_ff1 = nrm(ks[10], (DEPTH, D_MODEL, D_FF), D_MODEL ** -0.5)
    w_ff2 = nrm(ks[11], (DEPTH, D_FF, D_MODEL), D_FF ** -0.5)
    final_norm_w = 1.0 + nrm(ks[12], (D_MODEL,), 0.02)
    return {"x": x, "norm1_w": norm1_w, "w_in": w_in, "conv_w": conv_w, "conv_b": conv_b,
            "gate_b": gate_b, "mlstm_norm_w": mlstm_norm_w, "rpb": rpb, "w_out": w_out,
            "norm2_w": norm2_w, "w_ff1": w_ff1, "w_ff2": w_ff2, "final_norm_w": final_norm_w}


def reference(x, norm1_w, w_in, conv_w, conv_b, gate_b, mlstm_norm_w, rpb, w_out,
              norm2_w, w_ff1, w_ff2, final_norm_w):
    splits = np.cumsum([D_MLSTM, D_MLSTM, D_MLSTM, D_MLSTM, N_GATES, D_NA, D_NA]).tolist()
    for l in range(DEPTH):
        h = rms_norm(x, norm1_w[l])
        proj = h @ w_in[l]
        q_m, k_m, v_m, o_m, gates, q_n, k_n, v_n = jnp.split(proj, splits, axis=-1)
        qk = jax.nn.silu(centered_dwconv(jnp.concatenate([q_m, k_m], axis=-1), conv_w[l], conv_b[l]))
        q_m, k_m = qk[..., :D_MLSTM], qk[..., D_MLSTM:]
        y_m = mlstm_mixer(q_m, k_m, v_m, o_m, gates, gate_b[l], mlstm_norm_w[l])
        y_n = neighbourhood_attention(q_n, k_n, v_n, rpb[l])
        y = jnp.concatenate([y_m.astype(x.dtype), y_n.astype(x.dtype)], axis=-1)
        x = x + y @ w_out[l]
        h = rms_norm(x, norm2_w[l])
        x = x + jnp.square(jax.nn.relu(h @ w_ff1[l])) @ w_ff2[l]
    return rms_norm(x, final_norm_w)
```

```python
import functools

import numpy as np
import jax
import jax.numpy as jnp
from jax import lax
from jax.experimental import pallas as pl
from jax.experimental.pallas import tpu as pltpu

EPS = 1e-6
GRID_W = 64
MLSTM_HEADS = 4
MLSTM_HD = 128
NA_HEADS = 8
NA_HD = 64
NA_KH = 8
NA_KW = 16
CONV_K = 3
N_GATES = 4 * MLSTM_HEADS
LCH = 128
GATE_PAD = 128
NEG = -1e30

BF16 = jnp.bfloat16
F32 = jnp.float32

VMEM_LIMIT = 56 * 1024 * 1024


def _dot(a, b):
    return jnp.dot(a, b, preferred_element_type=F32)


def _dot_nt(a, b):
    return lax.dot_general(a, b, (((1,), (1,)), ((), ())), preferred_element_type=F32)


def _rms(x, w):
    return x * lax.rsqrt(jnp.mean(x * x, axis=-1, keepdims=True) + EPS) * w


def _in_proj_kernel(x_ref, nw_ref, w_ref, qk_ref, v_ref, o_ref, n_ref, g_ref, *, d_m, d_n3):
    hn = _rms(x_ref[...], nw_ref[...]).astype(BF16)
    c0 = 2 * d_m
    qk_ref[...] = _dot(hn, w_ref[:, 0:c0])
    v_ref[...] = _dot(hn, w_ref[:, c0:c0 + d_m]).astype(BF16)
    o_ref[...] = _dot(hn, w_ref[:, c0 + d_m:c0 + 2 * d_m]).astype(BF16)
    c1 = c0 + 2 * d_m
    n_ref[...] = _dot(hn, w_ref[:, c1:c1 + d_n3]).astype(BF16)
    g = _dot(hn, w_ref[:, c1 + d_n3:c1 + d_n3 + GATE_PAD])
    g_ref[...] = g.T[0:N_GATES, :]


def _in_proj(x2, nw, wp, *, d_m, d_n3, tm):
    n, d = x2.shape
    kern = functools.partial(_in_proj_kernel, d_m=d_m, d_n3=d_n3)
    return pl.pallas_call(
        kern,
        grid=(n // tm,),
        in_specs=[
            pl.BlockSpec((tm, d), lambda i: (i, 0)),
            pl.BlockSpec(memory_space=pltpu.VMEM),
            pl.BlockSpec(memory_space=pltpu.VMEM),
        ],
        out_specs=[
            pl.BlockSpec((tm, 2 * d_m), lambda i: (i, 0)),
            pl.BlockSpec((tm, d_m), lambda i: (i, 0)),
            pl.BlockSpec((tm, d_m), lambda i: (i, 0)),
            pl.BlockSpec((tm, d_n3), lambda i: (i, 0)),
            pl.BlockSpec((N_GATES, tm), lambda i: (0, i)),
        ],
        out_shape=[
            jax.ShapeDtypeStruct((n, 2 * d_m), F32),
            jax.ShapeDtypeStruct((n, d_m), BF16),
            jax.ShapeDtypeStruct((n, d_m), BF16),
            jax.ShapeDtypeStruct((n, d_n3), BF16),
            jax.ShapeDtypeStruct((N_GATES, n), F32),
        ],
        compiler_params=pltpu.CompilerParams(
            dimension_semantics=("arbitrary",), vmem_limit_bytes=VMEM_LIMIT),
        name="in_proj",
    )(x2, nw, wp)


_WAF, _WAB, _GF, _GB, _LFF, _LFB, _TOTF, _MLF, _TOTB, _MLB, _MPF, _MPB = range(12)


def _cumsum_lanes(x):
    lane = lax.broadcasted_iota(jnp.int32, x.shape, 1)
    k = 1
    while k < x.shape[1]:
        x = x + jnp.where(lane >= k, pltpu.roll(x, k, axis=1), 0.0)
        k *= 2
    return x


def _mlstm_kernel(gb_ref, q_ref, k_ref, v_ref, o_ref, g_ref, cwq_ref, cwk_ref, cbq_ref, cbk_ref,
                  nw_ref, y_ref, qs, ks, kt, vaug, sf, sb, stf, stb, gs, *, nch):
    hd = pl.program_id(1)
    seq = nch * LCH
    dh = MLSTM_HD

    i_f = g_ref[0, 0, 0] + gb_ref[0, hd]
    f_f = g_ref[1, 0, 0] + gb_ref[1, hd]
    i_b = g_ref[2, 0, 0] + gb_ref[2, hd]
    f_b = g_ref[3, 0, 0] + gb_ref[3, hd]

    def logsig(f):
        return jnp.minimum(f, 0.0) - jnp.log1p(jnp.exp(-jnp.abs(f)))

    lf_f = logsig(f_f)
    lf_b = logsig(f_b)
    tot_f = jnp.sum(lf_f, axis=1, keepdims=True)
    tot_b = jnp.sum(lf_b, axis=1, keepdims=True)
    b_f = _cumsum_lanes(lf_f)
    b_b = tot_b - _cumsum_lanes(lf_b) + lf_b
    a_f = tot_f - b_f + i_f
    a_b = tot_b - b_b + i_b
    ml_f = jnp.max(a_f, axis=1, keepdims=True)
    ml_b = jnp.max(a_b, axis=1, keepdims=True)
    shp = (nch, LCH)
    gs[_WAF] = jnp.exp(a_f - ml_f)
    gs[_WAB] = jnp.exp(a_b - ml_b)
    gs[_GF] = i_f - b_f
    gs[_GB] = i_b - b_b
    gs[_LFF] = lf_f
    gs[_LFB] = lf_b
    gs[_TOTF] = jnp.broadcast_to(tot_f, shp)
    gs[_MLF] = jnp.broadcast_to(ml_f, shp)
    gs[_TOTB] = jnp.broadcast_to(tot_b, shp)
    gs[_MLB] = jnp.broadcast_to(ml_b, shp)

    row = lax.broadcasted_iota(jnp.int32, (LCH, dh), 0)
    ones = jnp.ones((LCH, dh), BF16)

    def conv_act(ref, cw_ref, cb_ref, c):
        r0 = pl.multiple_of(c * LCH, LCH)
        x = ref[0, pl.ds(r0, LCH), :]
        prev = ref[0, pl.ds(jnp.maximum(r0 - 1, 0), 1), :] * jnp.where(c > 0, 1.0, 0.0)
        nxt = ref[0, pl.ds(jnp.minimum(r0 + LCH, seq - 1), 1), :] * jnp.where(c < nch - 1, 1.0, 0.0)
        xm1 = jnp.where(row == 0, prev, pltpu.roll(x, 1, axis=0))
        xp1 = jnp.where(row == LCH - 1, nxt, pltpu.roll(x, LCH - 1, axis=0))
        y = cw_ref[0:1, :] * xm1 + cw_ref[1:2, :] * x + cw_ref[2:3, :] * xp1 + cb_ref[...]
        return y * jax.nn.sigmoid(y)

    def prep_body(c, carry):
        r0 = pl.multiple_of(c * LCH, LCH)
        qs[pl.ds(r0, LCH), :] = conv_act(q_ref, cwq_ref, cbq_ref, c).astype(BF16)
        ka = conv_act(k_ref, cwk_ref, cbk_ref, c) * (dh ** -0.5)
        ks[pl.ds(r0, LCH), :] = ka.astype(BF16)
        kt[c] = ka.T
        vaug[pl.ds(r0, LCH), 0:dh] = v_ref[0, pl.ds(r0, LCH), :]
        vaug[pl.ds(r0, LCH), dh:2 * dh] = ones
        return carry

    lax.fori_loop(0, nch, prep_body, 0)

    stf[...] = jnp.zeros_like(stf)
    stb[...] = jnp.zeros_like(stb)

    def scan_dir(c, m_prev, st, s_out, wa_i, tot_i, ml_i, mp_i):
        r0 = pl.multiple_of(c * LCH, LCH)
        tot = gs[tot_i, pl.ds(c, 1), :]
        ml = gs[ml_i, pl.ds(c, 1), :]
        gs[mp_i, pl.ds(c, 1), :] = m_prev
        s_prev = st[...]
        s_out[c] = s_prev.astype(BF16)
        a = (kt[c] * gs[wa_i, pl.ds(c, 1), :]).astype(BF16)
        u = _dot(a, vaug[pl.ds(r0, LCH), :])
        m_new = jnp.maximum(tot + m_prev, ml)
        s_old = jnp.exp(tot + m_prev - m_new)
        s_loc = jnp.exp(ml - m_new)
        s_old2 = jnp.concatenate([s_old, s_old], axis=1)
        s_loc2 = jnp.concatenate([s_loc, s_loc], axis=1)
        st[...] = s_old2 * s_prev + s_loc2 * u
        return m_new

    def scan_body(i, carry):
        m_f, m_b = carry
        m_f = scan_dir(i, m_f, stf, sf, _WAF, _TOTF, _MLF, _MPF)
        m_b = scan_dir(nch - 1 - i, m_b, stb, sb, _WAB, _TOTB, _MLB, _MPB)
        return m_f, m_b

    zero_row = jnp.zeros((1, LCH), F32)
    lax.fori_loop(0, nch, scan_body, (zero_row, zero_row))

    jj = lax.broadcasted_iota(jnp.int32, (LCH, LCH), 0)
    ss = lax.broadcasted_iota(jnp.int32, (LCH, LCH), 1)
    mask_f = ss <= jj
    mask_b = ss >= jj

    def out_dir(c, qc, qk, vc, mask, s_ref, g_i, lf_i, mp_i):
        g_row = gs[g_i, pl.ds(c, 1), :]
        lf_row = gs[lf_i, pl.ds(c, 1), :]
        m_prev = gs[mp_i, pl.ds(c, 1), :]
        gm = jnp.where(mask, g_row, -jnp.inf)
        mj = jnp.maximum(jnp.max(gm, axis=1, keepdims=True), m_prev)
        e = jnp.exp(gm - mj)
        bj = jnp.sum(jnp.where(mask, lf_row, 0.0), axis=1, keepdims=True)
        p = (e * qk).astype(BF16)
        inter = _dot(qc, s_ref[c])
        intra = _dot(p, vc)
        w_inter = jnp.exp(m_prev - mj)
        num = w_inter * inter[:, 0:dh] + intra[:, 0:dh]
        den = w_inter * inter[:, dh:2 * dh] + intra[:, dh:2 * dh]
        return num / jnp.maximum(jnp.abs(den), jnp.exp(-bj - mj))

    def out_body(c, carry):
        r0 = pl.multiple_of(c * LCH, LCH)
        qc = qs[pl.ds(r0, LCH), :]
        kc = ks[pl.ds(r0, LCH), :]
        vc = vaug[pl.ds(r0, LCH), :]
        qk = _dot_nt(qc, kc)
        h = (out_dir(c, qc, qk, vc, mask_f, sf, _GF, _LFF, _MPF)
             + out_dir(c, qc, qk, vc, mask_b, sb, _GB, _LFB, _MPB))
        hn = h * lax.rsqrt(jnp.mean(h * h, axis=-1, keepdims=True) + EPS) * nw_ref[...]
        o = o_ref[0, pl.ds(r0, LCH), :].astype(F32)
        y_ref[0, pl.ds(r0, LCH), :] = (jax.nn.sigmoid(o) * hn).astype(BF16)
        return carry

    lax.fori_loop(0, nch, out_body, 0)


def _mlstm(gate_b, qk, v, o, gates5, conv_w, conv_b, norm_w):
    bsz, seq, d_m = v.shape
    nh, dh = MLSTM_HEADS, MLSTM_HD
    nch = seq // LCH
    kern = functools.partial(_mlstm_kernel, nch=nch)
    seq_blk = lambda off: pl.BlockSpec((1, seq, dh), lambda b, h: (b, 0, h + off))
    vec_blk = lambda rows, off: pl.BlockSpec((rows, dh), lambda b, h: (0, h + off))
    return pl.pallas_call(
        kern,
        grid=(bsz, nh),
        in_specs=[
            pl.BlockSpec(memory_space=pltpu.SMEM),
            seq_blk(0), seq_blk(nh),
            seq_blk(0), seq_blk(0),
            pl.BlockSpec((4, 1, 1, nch, LCH), lambda b, h: (0, h, b, 0, 0)),
            vec_blk(CONV_K, 0), vec_blk(CONV_K, nh),
            vec_blk(1, 0), vec_blk(1, nh),
            vec_blk(1, 0),
        ],
        out_specs=pl.BlockSpec((1, seq, dh), lambda b, h: (b, 0, h)),
        out_shape=jax.ShapeDtypeStruct((bsz, seq, d_m), BF16),
        scratch_shapes=[
            pltpu.VMEM((seq, dh), BF16),
            pltpu.VMEM((seq, dh), BF16),
            pltpu.VMEM((nch, dh, LCH), F32),
            pltpu.VMEM((seq, 2 * dh), BF16),
            pltpu.VMEM((nch, dh, 2 * dh), BF16),
            pltpu.VMEM((nch, dh, 2 * dh), BF16),
            pltpu.VMEM((dh, 2 * dh), F32),
            pltpu.VMEM((dh, 2 * dh), F32),
            pltpu.VMEM((12, nch, LCH), F32),
        ],
        compiler_params=pltpu.CompilerParams(
            dimension_semantics=("arbitrary", "arbitrary"), vmem_limit_bytes=VMEM_LIMIT),
        name="mlstm",
    )(gate_b, qk, qk, v, o, gates5, conv_w, conv_w, conv_b, conv_b, norm_w)


NA_ROWS_PER_STEP = 8


def _natten_kernel(q_ref, k_ref, v_ref, bias_ref, o_ref, *, rows):
    rb = pl.program_id(1)
    band = NA_KH * GRID_W
    lane = lax.broadcasted_iota(jnp.int32, (GRID_W, 2 * NA_HD), 1)
    lo = lane < NA_HD

    def row_body(i, carry):
        r = rb * NA_ROWS_PER_STEP + i
        rs = jnp.clip(r - NA_KH // 2, 0, rows - NA_KH)
        d = r - rs
        q0 = pl.multiple_of(i * GRID_W, GRID_W)
        k0 = pl.multiple_of(rs * GRID_W, GRID_W)
        for p in range(NA_HEADS // 2):
            cs = slice(p * 2 * NA_HD, (p + 1) * 2 * NA_HD)
            qp = q_ref[0, pl.ds(q0, GRID_W), cs].astype(F32)
            qq = jnp.concatenate([jnp.where(lo, qp, 0.0), jnp.where(lo, 0.0, qp)], axis=0).astype(BF16)
            s = _dot_nt(qq, k_ref[0, pl.ds(k0, band), cs]) + bias_ref[p, d]
            m = jnp.max(s, axis=-1, keepdims=True)
            e = jnp.exp(s - m)
            l = jnp.sum(e, axis=-1, keepdims=True)
            o2 = _dot(e.astype(BF16), v_ref[0, pl.ds(k0, band), cs]) / l
            op = jnp.where(lo, o2[0:GRID_W], o2[GRID_W:2 * GRID_W])
            o_ref[0, pl.ds(q0, GRID_W), cs] = op.astype(BF16)
        return carry

    lax.fori_loop(0, NA_ROWS_PER_STEP, row_body, 0)


def _natten(qkv, bias):
    bsz, seq, d3 = qkv.shape
    d_n = d3 // 3
    rows = seq // GRID_W
    tq = NA_ROWS_PER_STEP * GRID_W
    kern = functools.partial(_natten_kernel, rows=rows)
    return pl.pallas_call(
        kern,
        grid=(bsz, rows // NA_ROWS_PER_STEP),
        in_specs=[
            pl.BlockSpec((1, tq, d_n), lambda b, r: (b, r, 0)),
            pl.BlockSpec((1, seq, d_n), lambda b, r: (b, 0, 1)),
            pl.BlockSpec((1, seq, d_n), lambda b, r: (b, 0, 2)),
            pl.BlockSpec(memory_space=pltpu.VMEM),
        ],
        out_specs=pl.BlockSpec((1, tq, d_n), lambda b, r: (b, r, 0)),
        out_shape=jax.ShapeDtypeStruct((bsz, seq, d_n), BF16),
        compiler_params=pltpu.CompilerParams(
            dimension_semantics=("arbitrary", "arbitrary"), vmem_limit_bytes=VMEM_LIMIT),
        name="natten",
    )(qkv, qkv, qkv, bias)


def _natten_bias(rpb, rows):
    cols = np.arange(GRID_W)
    c_start = np.clip(cols - NA_KW // 2, 0, GRID_W - NA_KW)
    kc = np.arange(GRID_W)
    valid = (kc[None, :] >= c_start[:, None]) & (kc[None, :] < c_start[:, None] + NA_KW)
    rel_c = np.clip(kc[None, :] - cols[:, None] + NA_KW - 1, 0, 2 * NA_KW - 2)
    d = np.arange(NA_KH)
    i = np.arange(NA_KH)
    rel_r = i[None, :] - d[:, None] + NA_KH - 1
    b = rpb[:, rel_r[:, None, :, None], rel_c[None, :, None, :]]
    b = jnp.where(valid[None, None, :, None, :], b.astype(F32), NEG)
    nh = rpb.shape[0]
    b = b.reshape(nh // 2, 2, NA_KH, GRID_W, NA_KH * GRID_W)
    return b.transpose(0, 2, 1, 3, 4).reshape(nh // 2, NA_KH, 2 * GRID_W, NA_KH * GRID_W)


FF_CHUNK = 1024


def _out_ffn_kernel(x_ref, ym_ref, yn_ref, wo_ref, nw_ref, w1_ref, w2_ref, fw_ref, o_ref, x1_ref, *,
                    final):
    d_m = ym_ref.shape[1]
    x1_ref[...] = x_ref[...] + _dot(ym_ref[...], wo_ref[0:d_m, :]) + _dot(yn_ref[...], wo_ref[d_m:, :])
    h = _rms(x1_ref[...], nw_ref[...]).astype(BF16)
    d_ff = w1_ref.shape[1]
    ffn = None
    for j in range(d_ff // FF_CHUNK):
        cs = slice(j * FF_CHUNK, (j + 1) * FF_CHUNK)
        hid = jnp.square(jnp.maximum(_dot(h, w1_ref[:, cs]), 0.0)).astype(BF16)
        part = _dot(hid, w2_ref[cs, :])
        ffn = part if ffn is None else ffn + part
    acc = x1_ref[...] + ffn
    if final:
        acc = _rms(acc, fw_ref[...])
    o_ref[...] = acc


def _out_ffn(x2, ym, yn, wo, nw, w1, w2, fw, *, final, tm):
    n, d = x2.shape
    d_m, d_n = ym.shape[1], yn.shape[1]
    kern = functools.partial(_out_ffn_kernel, final=final)
    resident = pl.BlockSpec(memory_space=pltpu.VMEM)
    return pl.pallas_call(
        kern,
        grid=(n // tm,),
        in_specs=[
            pl.BlockSpec((tm, d), lambda i: (i, 0)),
            pl.BlockSpec((tm, d_m), lambda i: (i, 0)),
            pl.BlockSpec((tm, d_n), lambda i: (i, 0)),
            resident, resident, resident, resident, resident,
        ],
        out_specs=pl.BlockSpec((tm, d), lambda i: (i, 0)),
        out_shape=jax.ShapeDtypeStruct((n, d), F32),
        scratch_shapes=[pltpu.VMEM((tm, d), F32)],
        compiler_params=pltpu.CompilerParams(
            dimension_semantics=("arbitrary",), vmem_limit_bytes=VMEM_LIMIT),
        name="out_ffn",
    )(x2, ym, yn, wo, nw, w1, w2, fw)


def kernel(x, norm1_w, w_in, conv_w, conv_b, gate_b, mlstm_norm_w, rpb, w_out, norm2_w, w_ff1, w_ff2,
           final_norm_w):
    bsz, seq, d = x.shape
    depth = w_in.shape[0]
    d_m = MLSTM_HEADS * MLSTM_HD
    d_n = NA_HEADS * NA_HD
    n = bsz * seq
    rows = seq // GRID_W
    nch = seq // LCH
    assert w_in.shape[2] == 4 * d_m + N_GATES + 3 * d_n
    assert seq % LCH == 0 and rows % NA_ROWS_PER_STEP == 0 and rows >= NA_KH

    g0 = 4 * d_m
    n0 = g0 + N_GATES
    x2 = x.reshape(n, d)
    for l in range(depth):
        w = w_in[l]
        wp = jnp.concatenate([
            w[:, 0:g0], w[:, n0:n0 + d_n] * (NA_HD ** -0.5), w[:, n0 + d_n:],
            jnp.pad(w[:, g0:n0], ((0, 0), (0, GATE_PAD - N_GATES)))], axis=1).astype(BF16)
        qk, v, o, qkv_n, gates_t = _in_proj(x2, norm1_w[l][None], wp, d_m=d_m, d_n3=3 * d_n, tm=512)

        y_m = _mlstm(
            gate_b[l].reshape(4, MLSTM_HEADS),
            qk.reshape(bsz, seq, 2 * d_m), v.reshape(bsz, seq, d_m), o.reshape(bsz, seq, d_m),
            gates_t.reshape(4, MLSTM_HEADS, bsz, nch, LCH),
            conv_w[l], conv_b[l][None], mlstm_norm_w[l][None])
        y_n = _natten(qkv_n.reshape(bsz, seq, 3 * d_n), _natten_bias(rpb[l], rows))

        x2 = _out_ffn(
            x2, y_m.reshape(n, d_m), y_n.reshape(n, d_n), w_out[l].astype(BF16), norm2_w[l][None],
            w_ff1[l].astype(BF16), w_ff2[l].astype(BF16), final_norm_w[None],
            final=(l == depth - 1), tm=512)
    return x2.reshape(bsz, seq, d)
```

```python
import functools

import numpy as np
import jax
import jax.numpy as jnp
from jax import lax
from jax.experimental import pallas as pl
from jax.experimental.pallas import tpu as pltpu

EPS = 1e-6
GRID_W = 64
MLSTM_HEADS = 4
MLSTM_HD = 128
NA_HEADS = 8
NA_HD = 64
NA_KH = 8
NA_KW = 16
CONV_K = 3
N_GATES = 4 * MLSTM_HEADS
LCH = 128
GATE_PAD = 128
NEG = -1e30
LOG2E = 1.4426950408889634

BF16 = jnp.bfloat16
F32 = jnp.float32

VMEM_LIMIT = 56 * 1024 * 1024


def _dot(a, b):
    return jnp.dot(a, b, preferred_element_type=F32)


def _dot_nt(a, b):
    return lax.dot_general(a, b, (((1,), (1,)), ((), ())), preferred_element_type=F32)


def _rms(x, w):
    return x * lax.rsqrt(jnp.mean(x * x, axis=-1, keepdims=True) + EPS) * w


def _in_proj_kernel(x_ref, nw_ref, w_ref, qk_ref, v_ref, o_ref, n_ref, g_ref, *, d_m, d_n3):
    hn = _rms(x_ref[...], nw_ref[...]).astype(BF16)
    c0 = 2 * d_m
    qk_ref[...] = _dot(hn, w_ref[:, 0:c0])
    v_ref[...] = _dot(hn, w_ref[:, c0:c0 + d_m]).astype(BF16)
    o_ref[...] = _dot(hn, w_ref[:, c0 + d_m:c0 + 2 * d_m]).astype(BF16)
    c1 = c0 + 2 * d_m
    n_ref[...] = _dot(hn, w_ref[:, c1:c1 + d_n3]).astype(BF16)
    g = _dot(hn, w_ref[:, c1 + d_n3:c1 + d_n3 + GATE_PAD])
    g_ref[...] = g.T[0:N_GATES, :]


def _in_proj(x2, nw, wp, *, d_m, d_n3, tm):
    n, d = x2.shape
    kern = functools.partial(_in_proj_kernel, d_m=d_m, d_n3=d_n3)
    return pl.pallas_call(
        kern,
        grid=(n // tm,),
        in_specs=[
            pl.BlockSpec((tm, d), lambda i: (i, 0)),
            pl.BlockSpec(memory_space=pltpu.VMEM),
            pl.BlockSpec(memory_space=pltpu.VMEM),
        ],
        out_specs=[
            pl.BlockSpec((tm, 2 * d_m), lambda i: (i, 0)),
            pl.BlockSpec((tm, d_m), lambda i: (i, 0)),
            pl.BlockSpec((tm, d_m), lambda i: (i, 0)),
            pl.BlockSpec((tm, d_n3), lambda i: (i, 0)),
            pl.BlockSpec((N_GATES, tm), lambda i: (0, i)),
        ],
        out_shape=[
            jax.ShapeDtypeStruct((n, 2 * d_m), F32),
            jax.ShapeDtypeStruct((n, d_m), BF16),
            jax.ShapeDtypeStruct((n, d_m), BF16),
            jax.ShapeDtypeStruct((n, d_n3), BF16),
            jax.ShapeDtypeStruct((N_GATES, n), F32),
        ],
        compiler_params=pltpu.CompilerParams(
            dimension_semantics=("arbitrary",), vmem_limit_bytes=VMEM_LIMIT),
        name="in_proj",
    )(x2, nw, wp)


_WAF, _WAB, _GF, _GB, _LFF, _LFB, _TOTF, _MLF, _TOTB, _MLB, _MPF, _MPB = range(12)


def _cumsum_lanes(x):
    lane = lax.broadcasted_iota(jnp.int32, x.shape, 1)
    k = 1
    while k < x.shape[1]:
        x = x + jnp.where(lane >= k, pltpu.roll(x, k, axis=1), 0.0)
        k *= 2
    return x


def _mlstm_kernel(gb_ref, q_ref, k_ref, v_ref, o_ref, g_ref, cwq_ref, cwk_ref, cbq_ref, cbk_ref,
                  nw_ref, y_ref, qs, ks, kt, vaug, sf, sb, stf, stb, gs, *, nch):
    hd = pl.program_id(1)
    seq = nch * LCH
    dh = MLSTM_HD

    i_f = g_ref[0, 0, 0] + gb_ref[0, hd]
    f_f = g_ref[1, 0, 0] + gb_ref[1, hd]
    i_b = g_ref[2, 0, 0] + gb_ref[2, hd]
    f_b = g_ref[3, 0, 0] + gb_ref[3, hd]

    def logsig(f):
        return jnp.minimum(f, 0.0) - jnp.log1p(jnp.exp(-jnp.abs(f)))

    lf_f = logsig(f_f)
    lf_b = logsig(f_b)
    tot_f = jnp.sum(lf_f, axis=1, keepdims=True)
    tot_b = jnp.sum(lf_b, axis=1, keepdims=True)
    b_f = _cumsum_lanes(lf_f)
    b_b = tot_b - _cumsum_lanes(lf_b) + lf_b
    a_f = tot_f - b_f + i_f
    a_b = tot_b - b_b + i_b
    ml_f = jnp.max(a_f, axis=1, keepdims=True)
    ml_b = jnp.max(a_b, axis=1, keepdims=True)
    shp = (nch, LCH)
    gs[_WAF] = jnp.exp(a_f - ml_f)
    gs[_WAB] = jnp.exp(a_b - ml_b)
    gs[_GF] = i_f - b_f
    gs[_GB] = i_b - b_b
    gs[_LFF] = lf_f
    gs[_LFB] = lf_b
    gs[_TOTF] = jnp.broadcast_to(tot_f, shp)
    gs[_MLF] = jnp.broadcast_to(ml_f, shp)
    gs[_TOTB] = jnp.broadcast_to(tot_b, shp)
    gs[_MLB] = jnp.broadcast_to(ml_b, shp)

    row = lax.broadcasted_iota(jnp.int32, (LCH, dh), 0)
    ones = jnp.ones((LCH, dh), BF16)

    def conv_act(ref, cw_ref, cb_ref, c):
        r0 = pl.multiple_of(c * LCH, LCH)
        x = ref[0, pl.ds(r0, LCH), :]
        prev = ref[0, pl.ds(jnp.maximum(r0 - 1, 0), 1), :] * jnp.where(c > 0, 1.0, 0.0)
        nxt = ref[0, pl.ds(jnp.minimum(r0 + LCH, seq - 1), 1), :] * jnp.where(c < nch - 1, 1.0, 0.0)
        xm1 = jnp.where(row == 0, prev, pltpu.roll(x, 1, axis=0))
        xp1 = jnp.where(row == LCH - 1, nxt, pltpu.roll(x, LCH - 1, axis=0))
        y = cw_ref[0:1, :] * xm1 + cw_ref[1:2, :] * x + cw_ref[2:3, :] * xp1 + cb_ref[...]
        return y * jax.nn.sigmoid(y)

    def prep_body(c, carry):
        r0 = pl.multiple_of(c * LCH, LCH)
        qs[pl.ds(r0, LCH), :] = conv_act(q_ref, cwq_ref, cbq_ref, c).astype(BF16)
        ka = conv_act(k_ref, cwk_ref, cbk_ref, c) * (dh ** -0.5)
        ks[pl.ds(r0, LCH), :] = ka.astype(BF16)
        kt[c] = ka.T
        vaug[pl.ds(r0, LCH), 0:dh] = v_ref[0, pl.ds(r0, LCH), :]
        vaug[pl.ds(r0, LCH), dh:2 * dh] = ones
        return carry

    lax.fori_loop(0, nch, prep_body, 0, unroll=2)

    stf[...] = jnp.zeros_like(stf)
    stb[...] = jnp.zeros_like(stb)

    def scan_dir(c, m_prev, st, s_out, wa_i, tot_i, ml_i, mp_i):
        r0 = pl.multiple_of(c * LCH, LCH)
        tot = gs[tot_i, pl.ds(c, 1), :]
        ml = gs[ml_i, pl.ds(c, 1), :]
        gs[mp_i, pl.ds(c, 1), :] = m_prev
        s_prev = st[...]
        s_out[c] = s_prev.astype(BF16)
        a = (kt[c] * gs[wa_i, pl.ds(c, 1), :]).astype(BF16)
        u = _dot(a, vaug[pl.ds(r0, LCH), :])
        m_new = jnp.maximum(tot + m_prev, ml)
        s_old = jnp.exp(tot + m_prev - m_new)
        s_loc = jnp.exp(ml - m_new)
        s_old2 = jnp.concatenate([s_old, s_old], axis=1)
        s_loc2 = jnp.concatenate([s_loc, s_loc], axis=1)
        st[...] = s_old2 * s_prev + s_loc2 * u
        return m_new

    def scan_body(i, carry):
        m_f, m_b = carry
        m_f = scan_dir(i, m_f, stf, sf, _WAF, _TOTF, _MLF, _MPF)
        m_b = scan_dir(nch - 1 - i, m_b, stb, sb, _WAB, _TOTB, _MLB, _MPB)
        return m_f, m_b

    zero_row = jnp.zeros((1, LCH), F32)
    lax.fori_loop(0, nch, scan_body, (zero_row, zero_row), unroll=4)

    jj = lax.broadcasted_iota(jnp.int32, (LCH, LCH), 0)
    ss = lax.broadcasted_iota(jnp.int32, (LCH, LCH), 1)
    mask_f = ss <= jj
    mask_b = ss >= jj

    def out_dir(c, qc, qk, vc, mask, s_ref, g_i, lf_i, mp_i):
        g_row = gs[g_i, pl.ds(c, 1), :]
        lf_row = gs[lf_i, pl.ds(c, 1), :]
        m_prev = gs[mp_i, pl.ds(c, 1), :]
        gm = jnp.where(mask, g_row, -jnp.inf)
        mj = jnp.maximum(jnp.max(gm, axis=1, keepdims=True), m_prev)
        e = jnp.exp(gm - mj)
        bj = jnp.sum(jnp.where(mask, lf_row, 0.0), axis=1, keepdims=True)
        p = (e * qk).astype(BF16)
        inter = _dot(qc, s_ref[c])
        intra = _dot(p, vc)
        w_inter = jnp.exp(m_prev - mj)
        num = w_inter * inter[:, 0:dh] + intra[:, 0:dh]
        den = w_inter * inter[:, dh:2 * dh] + intra[:, dh:2 * dh]
        return num / jnp.maximum(jnp.abs(den), jnp.exp(-bj - mj))

    def out_body(c, carry):
        r0 = pl.multiple_of(c * LCH, LCH)
        qc = qs[pl.ds(r0, LCH), :]
        kc = ks[pl.ds(r0, LCH), :]
        vc = vaug[pl.ds(r0, LCH), :]
        qk = _dot_nt(qc, kc)
        h = (out_dir(c, qc, qk, vc, mask_f, sf, _GF, _LFF, _MPF)
             + out_dir(c, qc, qk, vc, mask_b, sb, _GB, _LFB, _MPB))
        hn = h * lax.rsqrt(jnp.mean(h * h, axis=-1, keepdims=True) + EPS) * nw_ref[...]
        o = o_ref[0, pl.ds(r0, LCH), :].astype(F32)
        y_ref[0, pl.ds(r0, LCH), :] = (jax.nn.sigmoid(o) * hn).astype(BF16)
        return carry

    lax.fori_loop(0, nch, out_body, 0, unroll=4)


def _mlstm(gate_b, qk, v, o, gates5, conv_w, conv_b, norm_w):
    bsz, seq, d_m = v.shape
    nh, dh = MLSTM_HEADS, MLSTM_HD
    nch = seq // LCH
    kern = functools.partial(_mlstm_kernel, nch=nch)
    seq_blk = lambda off: pl.BlockSpec((1, seq, dh), lambda b, h: (b, 0, h + off))
    vec_blk = lambda rows, off: pl.BlockSpec((rows, dh), lambda b, h: (0, h + off))
    return pl.pallas_call(
        kern,
        grid=(bsz, nh),
        in_specs=[
            pl.BlockSpec(memory_space=pltpu.SMEM),
            seq_blk(0), seq_blk(nh),
            seq_blk(0), seq_blk(0),
            pl.BlockSpec((4, 1, 1, nch, LCH), lambda b, h: (0, h, b, 0, 0)),
            vec_blk(CONV_K, 0), vec_blk(CONV_K, nh),
            vec_blk(1, 0), vec_blk(1, nh),
            vec_blk(1, 0),
        ],
        out_specs=pl.BlockSpec((1, seq, dh), lambda b, h: (b, 0, h)),
        out_shape=jax.ShapeDtypeStruct((bsz, seq, d_m), BF16),
        scratch_shapes=[
            pltpu.VMEM((seq, dh), BF16),
            pltpu.VMEM((seq, dh), BF16),
            pltpu.VMEM((nch, dh, LCH), F32),
            pltpu.VMEM((seq, 2 * dh), BF16),
            pltpu.VMEM((nch, dh, 2 * dh), BF16),
            pltpu.VMEM((nch, dh, 2 * dh), BF16),
            pltpu.VMEM((dh, 2 * dh), F32),
            pltpu.VMEM((dh, 2 * dh), F32),
            pltpu.VMEM((12, nch, LCH), F32),
        ],
        compiler_params=pltpu.CompilerParams(
            dimension_semantics=("arbitrary", "arbitrary"), vmem_limit_bytes=VMEM_LIMIT),
        name="mlstm",
    )(gate_b, qk, qk, v, o, gates5, conv_w, conv_w, conv_b, conv_b, norm_w)


NA_ROWS_PER_STEP = 8


def _natten_kernel(q_ref, k_ref, v_ref, bias_ref, o_ref, *, rows):
    rb = pl.program_id(1)
    band = NA_KH * GRID_W
    lane = lax.broadcasted_iota(jnp.int32, (GRID_W, 2 * NA_HD), 1)
    lo = lane < NA_HD

    def row_body(i, carry):
        r = rb * NA_ROWS_PER_STEP + i
        rs = jnp.clip(r - NA_KH // 2, 0, rows - NA_KH)
        d = r - rs
        q0 = pl.multiple_of(i * GRID_W, GRID_W)
        k0 = pl.multiple_of(rs * GRID_W, GRID_W)
        for p in range(NA_HEADS // 2):
            cs = slice(p * 2 * NA_HD, (p + 1) * 2 * NA_HD)
            qp = q_ref[0, pl.ds(q0, GRID_W), cs].astype(F32)
            qq = jnp.concatenate([jnp.where(lo, qp, 0.0), jnp.where(lo, 0.0, qp)], axis=0).astype(BF16)
            s = _dot_nt(qq, k_ref[0, pl.ds(k0, band), cs]) + bias_ref[p, d]
            m = jnp.max(s, axis=-1, keepdims=True)
            e = jnp.exp2(s - m)
            l = jnp.sum(e, axis=-1, keepdims=True)
            o2 = _dot(e.astype(BF16), v_ref[0, pl.ds(k0, band), cs]) / l
            op = jnp.where(lo, o2[0:GRID_W], o2[GRID_W:2 * GRID_W])
            o_ref[0, pl.ds(q0, GRID_W), cs] = op.astype(BF16)
        return carry

    lax.fori_loop(0, NA_ROWS_PER_STEP, row_body, 0, unroll=2)


def _natten(qkv, bias):
    bsz, seq, d3 = qkv.shape
    d_n = d3 // 3
    rows = seq // GRID_W
    tq = NA_ROWS_PER_STEP * GRID_W
    kern = functools.partial(_natten_kernel, rows=rows)
    return pl.pallas_call(
        kern,
        grid=(bsz, rows // NA_ROWS_PER_STEP),
        in_specs=[
            pl.BlockSpec((1, tq, d_n), lambda b, r: (b, r, 0)),
            pl.BlockSpec((1, seq, d_n), lambda b, r: (b, 0, 1)),
            pl.BlockSpec((1, seq, d_n), lambda b, r: (b, 0, 2)),
            pl.BlockSpec(memory_space=pltpu.VMEM),
        ],
        out_specs=pl.BlockSpec((1, tq, d_n), lambda b, r: (b, r, 0)),
        out_shape=jax.ShapeDtypeStruct((bsz, seq, d_n), BF16),
        compiler_params=pltpu.CompilerParams(
            dimension_semantics=("arbitrary", "arbitrary"), vmem_limit_bytes=VMEM_LIMIT),
        name="natten",
    )(qkv, qkv, qkv, bias)


def _natten_bias(rpb):
    nh = rpb.shape[0]
    n_rr, n_rc = 2 * NA_KH - 1, 2 * NA_KW - 1
    cols = np.arange(GRID_W)
    c_start = np.clip(cols - NA_KW // 2, 0, GRID_W - NA_KW)
    kc = np.arange(GRID_W)
    valid = (kc[None, :] >= c_start[:, None]) & (kc[None, :] < c_start[:, None] + NA_KW)
    rel_c = kc[None, :] - cols[:, None] + NA_KW - 1
    onehot = ((rel_c[None] == np.arange(n_rc)[:, None, None]) & valid[None]).astype(np.float32)
    t = jnp.dot(rpb.astype(F32).reshape(nh * n_rr, n_rc), onehot.reshape(n_rc, GRID_W * GRID_W),
                precision=lax.Precision.HIGHEST).reshape(nh, n_rr, GRID_W, GRID_W)
    t = t * LOG2E + np.where(valid, 0.0, NEG).astype(np.float32)
    b = jnp.stack([t[:, NA_KH - 1 - d:2 * NA_KH - 1 - d] for d in range(NA_KH)], axis=1)
    b = b.reshape(nh // 2, 2, NA_KH, NA_KH, GRID_W, GRID_W).transpose(0, 2, 1, 4, 3, 5)
    return b.reshape(nh // 2, NA_KH, 2 * GRID_W, NA_KH * GRID_W)


FF_CHUNK = 1024


def _out_ffn_kernel(x_ref, ym_ref, yn_ref, wo_ref, nw_ref, w1_ref, w2_ref, fw_ref, o_ref, x1_ref, *,
                    final):
    d_m = ym_ref.shape[1]
    x1_ref[...] = x_ref[...] + _dot(ym_ref[...], wo_ref[0:d_m, :]) + _dot(yn_ref[...], wo_ref[d_m:, :])
    h = _rms(x1_ref[...], nw_ref[...]).astype(BF16)
    d_ff = w1_ref.shape[1]
    ffn = None
    for j in range(d_ff // FF_CHUNK):
        cs = slice(j * FF_CHUNK, (j + 1) * FF_CHUNK)
        hid = jnp.square(jnp.maximum(_dot(h, w1_ref[:, cs]), 0.0)).astype(BF16)
        part = _dot(hid, w2_ref[cs, :])
        ffn = part if ffn is None else ffn + part
    acc = x1_ref[...] + ffn
    if final:
        acc = _rms(acc, fw_ref[...])
    o_ref[...] = acc


def _out_ffn(x2, ym, yn, wo, nw, w1, w2, fw, *, final, tm):
    n, d = x2.shape
    d_m, d_n = ym.shape[1], yn.shape[1]
    kern = functools.partial(_out_ffn_kernel, final=final)
    resident = pl.BlockSpec(memory_space=pltpu.VMEM)
    return pl.pallas_call(
        kern,
        grid=(n // tm,),
        in_specs=[
            pl.BlockSpec((tm, d), lambda i: (i, 0)),
            pl.BlockSpec((tm, d_m), lambda i: (i, 0)),
            pl.BlockSpec((tm, d_n), lambda i: (i, 0)),
            resident, resident, resident, resident, resident,
        ],
        out_specs=pl.BlockSpec((tm, d), lambda i: (i, 0)),
        out_shape=jax.ShapeDtypeStruct((n, d), F32),
        scratch_shapes=[pltpu.VMEM((tm, d), F32)],
        compiler_params=pltpu.CompilerParams(
            dimension_semantics=("arbitrary",), vmem_limit_bytes=VMEM_LIMIT),
        name="out_ffn",
    )(x2, ym, yn, wo, nw, w1, w2, fw)


def kernel(x, norm1_w, w_in, conv_w, conv_b, gate_b, mlstm_norm_w, rpb, w_out, norm2_w, w_ff1, w_ff2,
           final_norm_w):
    bsz, seq, d = x.shape
    depth = w_in.shape[0]
    d_m = MLSTM_HEADS * MLSTM_HD
    d_n = NA_HEADS * NA_HD
    n = bsz * seq
    rows = seq // GRID_W
    nch = seq // LCH
    assert w_in.shape[2] == 4 * d_m + N_GATES + 3 * d_n
    assert seq % LCH == 0 and rows % NA_ROWS_PER_STEP == 0 and rows >= NA_KH

    g0 = 4 * d_m
    n0 = g0 + N_GATES
    x2 = x.reshape(n, d)
    for l in range(depth):
        w = w_in[l]
        wp = jnp.concatenate([
            w[:, 0:g0], w[:, n0:n0 + d_n] * (NA_HD ** -0.5 * LOG2E), w[:, n0 + d_n:],
            jnp.pad(w[:, g0:n0], ((0, 0), (0, GATE_PAD - N_GATES)))], axis=1).astype(BF16)
        qk, v, o, qkv_n, gates_t = _in_proj(x2, norm1_w[l][None], wp, d_m=d_m, d_n3=3 * d_n, tm=512)

        y_m = _mlstm(
            gate_b[l].reshape(4, MLSTM_HEADS),
            qk.reshape(bsz, seq, 2 * d_m), v.reshape(bsz, seq, d_m), o.reshape(bsz, seq, d_m),
            gates_t.reshape(4, MLSTM_HEADS, bsz, nch, LCH),
            conv_w[l], conv_b[l][None], mlstm_norm_w[l][None])
        y_n = _natten(qkv_n.reshape(bsz, seq, 3 * d_n), _natten_bias(rpb[l]))

        x2 = _out_ffn(
            x2, y_m.reshape(n, d_m), y_n.reshape(n, d_n), w_out[l].astype(BF16), norm2_w[l][None],
            w_ff1[l].astype(BF16), w_ff2[l].astype(BF16), final_norm_w[None],
            final=(l == depth - 1), tm=512)
    return x2.reshape(bsz, seq, d)
```

```python
import functools

import numpy as np
import jax
import jax.numpy as jnp
from jax import lax
from jax.experimental import pallas as pl
from jax.experimental.pallas import tpu as pltpu

EPS = 1e-6
GRID_W = 64
MLSTM_HEADS = 4
MLSTM_HD = 128
NA_HEADS = 8
NA_HD = 64
NA_KH = 8
NA_KW = 16
CONV_K = 3
N_GATES = 4 * MLSTM_HEADS
LCH = 128
OUT_CHUNKS_PER_ITER = 4
GATE_PAD = 128
NEG = -1e30
LOG2E = 1.4426950408889634

BF16 = jnp.bfloat16
F32 = jnp.float32

VMEM_LIMIT = 56 * 1024 * 1024


def _dot(a, b):
    return jnp.dot(a, b, preferred_element_type=F32)


def _dot_nt(a, b):
    return lax.dot_general(a, b, (((1,), (1,)), ((), ())), preferred_element_type=F32)


def _rms(x, w):
    return x * lax.rsqrt(jnp.mean(x * x, axis=-1, keepdims=True) + EPS) * w


def _in_proj_kernel(x_ref, nw_ref, w_ref, qk_ref, v_ref, o_ref, n_ref, g_ref, *, d_m, d_n3):
    hn = _rms(x_ref[...], nw_ref[...]).astype(BF16)
    c0 = 2 * d_m
    qk_ref[...] = _dot(hn, w_ref[:, 0:c0])
    v_ref[...] = _dot(hn, w_ref[:, c0:c0 + d_m]).astype(BF16)
    o_ref[...] = _dot(hn, w_ref[:, c0 + d_m:c0 + 2 * d_m]).astype(BF16)
    c1 = c0 + 2 * d_m
    n_ref[...] = _dot(hn, w_ref[:, c1:c1 + d_n3]).astype(BF16)
    g = _dot(hn, w_ref[:, c1 + d_n3:c1 + d_n3 + GATE_PAD])
    g_ref[...] = g.T[0:N_GATES, :]


def _in_proj(x2, nw, wp, *, d_m, d_n3, tm):
    n, d = x2.shape
    kern = functools.partial(_in_proj_kernel, d_m=d_m, d_n3=d_n3)
    return pl.pallas_call(
        kern,
        grid=(n // tm,),
        in_specs=[
            pl.BlockSpec((tm, d), lambda i: (i, 0)),
            pl.BlockSpec(memory_space=pltpu.VMEM),
            pl.BlockSpec(memory_space=pltpu.VMEM),
        ],
        out_specs=[
            pl.BlockSpec((tm, 2 * d_m), lambda i: (i, 0)),
            pl.BlockSpec((tm, d_m), lambda i: (i, 0)),
            pl.BlockSpec((tm, d_m), lambda i: (i, 0)),
            pl.BlockSpec((tm, d_n3), lambda i: (i, 0)),
            pl.BlockSpec((N_GATES, tm), lambda i: (0, i)),
        ],
        out_shape=[
            jax.ShapeDtypeStruct((n, 2 * d_m), F32),
            jax.ShapeDtypeStruct((n, d_m), BF16),
            jax.ShapeDtypeStruct((n, d_m), BF16),
            jax.ShapeDtypeStruct((n, d_n3), BF16),
            jax.ShapeDtypeStruct((N_GATES, n), F32),
        ],
        compiler_params=pltpu.CompilerParams(
            dimension_semantics=("arbitrary",), vmem_limit_bytes=VMEM_LIMIT),
        name="in_proj",
    )(x2, nw, wp)


_WAF, _WAB, _GF, _GB, _LFF, _LFB, _TOTF, _MLF, _TOTB, _MLB, _MPF, _MPB = range(12)


def _cumsum_lanes(x):
    lane = lax.broadcasted_iota(jnp.int32, x.shape, 1)
    k = 1
    while k < x.shape[1]:
        x = x + jnp.where(lane >= k, pltpu.roll(x, k, axis=1), 0.0)
        k *= 2
    return x


def _mlstm_kernel(gb_ref, q_ref, k_ref, v_ref, o_ref, g_ref, cwq_ref, cwk_ref, cbq_ref, cbk_ref,
                  nw_ref, y_ref, q32s, ks, kt, vaug, sf, sb, stf, stb, gs, *, nch):
    hd = pl.program_id(1)
    seq = nch * LCH
    dh = MLSTM_HD

    i_f = (g_ref[0, 0, 0] + gb_ref[0, hd]) * LOG2E
    f_f = g_ref[1, 0, 0] + gb_ref[1, hd]
    i_b = (g_ref[2, 0, 0] + gb_ref[2, hd]) * LOG2E
    f_b = g_ref[3, 0, 0] + gb_ref[3, hd]

    def logsig(f):
        return jnp.minimum(f, 0.0) - jnp.log1p(jnp.exp(-jnp.abs(f)))

    lf_f = logsig(f_f) * LOG2E
    lf_b = logsig(f_b) * LOG2E
    tot_f = jnp.sum(lf_f, axis=1, keepdims=True)
    tot_b = jnp.sum(lf_b, axis=1, keepdims=True)
    b_f = _cumsum_lanes(lf_f)
    b_b = tot_b - _cumsum_lanes(lf_b) + lf_b
    a_f = tot_f - b_f + i_f
    a_b = tot_b - b_b + i_b
    ml_f = jnp.max(a_f, axis=1, keepdims=True)
    ml_b = jnp.max(a_b, axis=1, keepdims=True)
    shp = (nch, LCH)
    gs[_WAF] = jnp.exp2(a_f - ml_f)
    gs[_WAB] = jnp.exp2(a_b - ml_b)
    gs[_GF] = i_f - b_f
    gs[_GB] = i_b - b_b
    gs[_LFF] = lf_f
    gs[_LFB] = lf_b
    gs[_TOTF] = jnp.broadcast_to(tot_f, shp)
    gs[_MLF] = jnp.broadcast_to(ml_f, shp)
    gs[_TOTB] = jnp.broadcast_to(tot_b, shp)
    gs[_MLB] = jnp.broadcast_to(ml_b, shp)

    row = lax.broadcasted_iota(jnp.int32, (LCH, dh), 0)
    ones = jnp.ones((LCH, dh), BF16)

    def conv_act(ref, cw_ref, cb_ref, c):
        r0 = pl.multiple_of(c * LCH, LCH)
        x = ref[0, pl.ds(r0, LCH), :]
        prev = ref[0, pl.ds(jnp.maximum(r0 - 1, 0), 1), :] * jnp.where(c > 0, 1.0, 0.0)
        nxt = ref[0, pl.ds(jnp.minimum(r0 + LCH, seq - 1), 1), :] * jnp.where(c < nch - 1, 1.0, 0.0)
        xm1 = jnp.where(row == 0, prev, pltpu.roll(x, 1, axis=0))
        xp1 = jnp.where(row == LCH - 1, nxt, pltpu.roll(x, LCH - 1, axis=0))
        y = cw_ref[0:1, :] * xm1 + cw_ref[1:2, :] * x + cw_ref[2:3, :] * xp1 + cb_ref[...]
        return y * jax.nn.sigmoid(y)

    def prep_body(c, carry):
        r0 = pl.multiple_of(c * LCH, LCH)
        q32s[pl.ds(r0, LCH), :] = conv_act(q_ref, cwq_ref, cbq_ref, c)
        ka = conv_act(k_ref, cwk_ref, cbk_ref, c) * (dh ** -0.5)
        ks[pl.ds(r0, LCH), :] = ka.astype(BF16)
        kt[c] = ka.T
        vaug[pl.ds(r0, LCH), 0:dh] = v_ref[0, pl.ds(r0, LCH), :]
        vaug[pl.ds(r0, LCH), dh:2 * dh] = ones
        return carry

    lax.fori_loop(0, nch, prep_body, 0, unroll=2)

    stf[...] = jnp.zeros_like(stf)
    stb[...] = jnp.zeros_like(stb)

    def scan_dir(c, m_prev, st, s_out, wa_i, tot_i, ml_i, mp_i):
        r0 = pl.multiple_of(c * LCH, LCH)
        tot = gs[tot_i, pl.ds(c, 1), :]
        ml = gs[ml_i, pl.ds(c, 1), :]
        gs[mp_i, pl.ds(c, 1), :] = m_prev
        s_prev = st[...]
        s_out[c] = s_prev.astype(BF16)
        a = (kt[c] * gs[wa_i, pl.ds(c, 1), :]).astype(BF16)
        u = _dot(a, vaug[pl.ds(r0, LCH), :])
        m_new = jnp.maximum(tot + m_prev, ml)
        s_old = jnp.exp2(tot + m_prev - m_new)
        s_loc = jnp.exp2(ml - m_new)
        s_old2 = jnp.concatenate([s_old, s_old], axis=1)
        s_loc2 = jnp.concatenate([s_loc, s_loc], axis=1)
        st[...] = s_old2 * s_prev + s_loc2 * u
        return m_new

    def scan_body(i, carry):
        m_f, m_b = carry
        m_f = scan_dir(i, m_f, stf, sf, _WAF, _TOTF, _MLF, _MPF)
        m_b = scan_dir(nch - 1 - i, m_b, stb, sb, _WAB, _TOTB, _MLB, _MPB)
        return m_f, m_b

    zero_row = jnp.zeros((1, LCH), F32)
    lax.fori_loop(0, nch, scan_body, (zero_row, zero_row), unroll=4)

    jj = lax.broadcasted_iota(jnp.int32, (LCH, LCH), 0)
    ss = lax.broadcasted_iota(jnp.int32, (LCH, LCH), 1)
    mask_f = ss <= jj
    mask_b = ss >= jj

    dirs = ((mask_f, sf, _GF, _LFF, _MPF), (mask_b, sb, _GB, _LFB, _MPB))

    def weights_dir(c, q32, qk, mask, g_i, lf_i, mp_i):
        g_row = gs[g_i, pl.ds(c, 1), :]
        lf_row = gs[lf_i, pl.ds(c, 1), :]
        m_prev = gs[mp_i, pl.ds(c, 1), :]
        gm = jnp.where(mask, g_row, -jnp.inf)
        mj = jnp.maximum(jnp.max(gm, axis=1, keepdims=True), m_prev)
        bj = jnp.sum(jnp.where(mask, lf_row, 0.0), axis=1, keepdims=True)
        p = (jnp.exp2(gm - mj) * qk).astype(BF16)
        wq = (jnp.exp2(m_prev - mj) * q32).astype(BF16)
        return jnp.concatenate([wq, p], axis=1), jnp.exp2(-bj - mj)

    def out_body(it, carry):
        cs = [it * OUT_CHUNKS_PER_ITER + j for j in range(OUT_CHUNKS_PER_ITER)]
        r0s = [pl.multiple_of(c * LCH, LCH) for c in cs]
        q32 = [q32s[pl.ds(r0, LCH), :] for r0 in r0s]
        qk = [_dot_nt(q.astype(BF16), ks[pl.ds(r0, LCH), :]) for q, r0 in zip(q32, r0s)]
        wts = [[weights_dir(c, q, s, mask, g_i, lf_i, mp_i) for (mask, _, g_i, lf_i, mp_i) in dirs]
               for c, q, s in zip(cs, q32, qk)]
        hs = []
        for c, r0, wt in zip(cs, r0s, wts):
            vc = vaug[pl.ds(r0, LCH), :]
            h = None
            for (lhs, bound), (_, s_ref, _, _, _) in zip(wt, dirs):
                tot = _dot(lhs, jnp.concatenate([s_ref[c], vc], axis=0))
                hd_ = tot[:, 0:dh] / jnp.maximum(jnp.abs(tot[:, dh:2 * dh]), bound)
                h = hd_ if h is None else h + hd_
            hs.append(h)
        for r0, h in zip(r0s, hs):
            hn = h * lax.rsqrt(jnp.mean(h * h, axis=-1, keepdims=True) + EPS) * nw_ref[...]
            o = o_ref[0, pl.ds(r0, LCH), :].astype(F32)
            y_ref[0, pl.ds(r0, LCH), :] = (jax.nn.sigmoid(o) * hn).astype(BF16)
        return carry

    lax.fori_loop(0, nch // OUT_CHUNKS_PER_ITER, out_body, 0)


def _mlstm(gate_b, qk, v, o, gates5, conv_w, conv_b, norm_w):
    bsz, seq, d_m = v.shape
    nh, dh = MLSTM_HEADS, MLSTM_HD
    nch = seq // LCH
    kern = functools.partial(_mlstm_kernel, nch=nch)
    seq_blk = lambda off: pl.BlockSpec((1, seq, dh), lambda b, h: (b, 0, h + off))
    vec_blk = lambda rows, off: pl.BlockSpec((rows, dh), lambda b, h: (0, h + off))
    return pl.pallas_call(
        kern,
        grid=(bsz, nh),
        in_specs=[
            pl.BlockSpec(memory_space=pltpu.SMEM),
            seq_blk(0), seq_blk(nh),
            seq_blk(0), seq_blk(0),
            pl.BlockSpec((4, 1, 1, nch, LCH), lambda b, h: (0, h, b, 0, 0)),
            vec_blk(CONV_K, 0), vec_blk(CONV_K, nh),
            vec_blk(1, 0), vec_blk(1, nh),
            vec_blk(1, 0),
        ],
        out_specs=pl.BlockSpec((1, seq, dh), lambda b, h: (b, 0, h)),
        out_shape=jax.ShapeDtypeStruct((bsz, seq, d_m), BF16),
        scratch_shapes=[
            pltpu.VMEM((seq, dh), F32),
            pltpu.VMEM((seq, dh), BF16),
            pltpu.VMEM((nch, dh, LCH), F32),
            pltpu.VMEM((seq, 2 * dh), BF16),
            pltpu.VMEM((nch, dh, 2 * dh), BF16),
            pltpu.VMEM((nch, dh, 2 * dh), BF16),
            pltpu.VMEM((dh, 2 * dh), F32),
            pltpu.VMEM((dh, 2 * dh), F32),
            pltpu.VMEM((12, nch, LCH), F32),
        ],
        compiler_params=pltpu.CompilerParams(
            dimension_semantics=("arbitrary", "arbitrary"), vmem_limit_bytes=VMEM_LIMIT),
        name="mlstm",
    )(gate_b, qk, qk, v, o, gates5, conv_w, conv_w, conv_b, conv_b, norm_w)


NA_ROWS_PER_STEP = 8
NA_ROWS_PER_ITER = 2


def _natten_kernel(q_ref, k_ref, v_ref, bias_ref, o_ref, s_scr, e_scr, l_scr, *, rows):
    rb = pl.program_id(1)
    band = NA_KH * GRID_W
    npair = NA_HEADS // 2
    lane = lax.broadcasted_iota(jnp.int32, (GRID_W, 2 * NA_HD), 1)
    lo = lane < NA_HD
    units = [(j, p) for j in range(NA_ROWS_PER_ITER) for p in range(npair)]

    def rows_body(it, carry):
        q0s, k0s, ds = [], [], []
        for j in range(NA_ROWS_PER_ITER):
            i = it * NA_ROWS_PER_ITER + j
            r = rb * NA_ROWS_PER_STEP + i
            rs = jnp.clip(r - NA_KH // 2, 0, rows - NA_KH)
            ds.append(r - rs)
            q0s.append(pl.multiple_of(i * GRID_W, GRID_W))
            k0s.append(pl.multiple_of(rs * GRID_W, GRID_W))
        for u, (j, p) in enumerate(units):
            cs = slice(p * 2 * NA_HD, (p + 1) * 2 * NA_HD)
            qp = q_ref[0, pl.ds(q0s[j], GRID_W), cs].astype(F32)
            qq = jnp.concatenate([jnp.where(lo, qp, 0.0), jnp.where(lo, 0.0, qp)], axis=0).astype(BF16)
            s_scr[u] = _dot_nt(qq, k_ref[0, pl.ds(k0s[j], band), cs]) + bias_ref[p, ds[j]]
        for u in range(len(units)):
            s = s_scr[u]
            e = jnp.exp2(s - jnp.max(s, axis=-1, keepdims=True))
            l_scr[u] = jnp.broadcast_to(jnp.sum(e, axis=-1, keepdims=True), l_scr.shape[1:])
            e_scr[u] = e.astype(BF16)
        for u, (j, p) in enumerate(units):
            cs = slice(p * 2 * NA_HD, (p + 1) * 2 * NA_HD)
            o2 = _dot(e_scr[u], v_ref[0, pl.ds(k0s[j], band), cs]) / l_scr[u]
            op = jnp.where(lo, o2[0:GRID_W], o2[GRID_W:2 * GRID_W])
            o_ref[0, pl.ds(q0s[j], GRID_W), cs] = op.astype(BF16)
        return carry

    lax.fori_loop(0, NA_ROWS_PER_STEP // NA_ROWS_PER_ITER, rows_body, 0)


def _natten(qkv, bias):
    bsz, seq, d3 = qkv.shape
    d_n = d3 // 3
    rows = seq // GRID_W
    tq = NA_ROWS_PER_STEP * GRID_W
    n_units = NA_ROWS_PER_ITER * NA_HEADS // 2
    kern = functools.partial(_natten_kernel, rows=rows)
    return pl.pallas_call(
        kern,
        grid=(bsz, rows // NA_ROWS_PER_STEP),
        in_specs=[
            pl.BlockSpec((1, tq, d_n), lambda b, r: (b, r, 0)),
            pl.BlockSpec((1, seq, d_n), lambda b, r: (b, 0, 1)),
            pl.BlockSpec((1, seq, d_n), lambda b, r: (b, 0, 2)),
            pl.BlockSpec(memory_space=pltpu.VMEM),
        ],
        out_specs=pl.BlockSpec((1, tq, d_n), lambda b, r: (b, r, 0)),
        out_shape=jax.ShapeDtypeStruct((bsz, seq, d_n), BF16),
        scratch_shapes=[
            pltpu.VMEM((n_units, 2 * GRID_W, NA_KH * GRID_W), F32),
            pltpu.VMEM((n_units, 2 * GRID_W, NA_KH * GRID_W), BF16),
            pltpu.VMEM((n_units, 2 * GRID_W, 2 * NA_HD), F32),
        ],
        compiler_params=pltpu.CompilerParams(
            dimension_semantics=("arbitrary", "arbitrary"), vmem_limit_bytes=VMEM_LIMIT),
        name="natten",
    )(qkv, qkv, qkv, bias)


def _natten_bias(rpb):
    nh = rpb.shape[0]
    n_rr, n_rc = 2 * NA_KH - 1, 2 * NA_KW - 1
    cols = np.arange(GRID_W)
    c_start = np.clip(cols - NA_KW // 2, 0, GRID_W - NA_KW)
    kc = np.arange(GRID_W)
    valid = (kc[None, :] >= c_start[:, None]) & (kc[None, :] < c_start[:, None] + NA_KW)
    rel_c = kc[None, :] - cols[:, None] + NA_KW - 1
    onehot = ((rel_c[None] == np.arange(n_rc)[:, None, None]) & valid[None]).astype(np.float32)
    t = jnp.dot(rpb.astype(F32).reshape(nh * n_rr, n_rc), onehot.reshape(n_rc, GRID_W * GRID_W),
                precision=lax.Precision.HIGHEST).reshape(nh, n_rr, GRID_W, GRID_W)
    t = t * LOG2E + np.where(valid, 0.0, NEG).astype(np.float32)
    b = jnp.stack([t[:, NA_KH - 1 - d:2 * NA_KH - 1 - d] for d in range(NA_KH)], axis=1)
    b = b.reshape(nh // 2, 2, NA_KH, NA_KH, GRID_W, GRID_W).transpose(0, 2, 1, 4, 3, 5)
    return b.reshape(nh // 2, NA_KH, 2 * GRID_W, NA_KH * GRID_W)


FF_CHUNK = 1024


def _out_ffn_kernel(x_ref, ym_ref, yn_ref, wo_ref, nw_ref, w1_ref, w2_ref, fw_ref, o_ref, x1_ref, *,
                    final):
    d_m = ym_ref.shape[1]
    x1_ref[...] = x_ref[...] + _dot(ym_ref[...], wo_ref[0:d_m, :]) + _dot(yn_ref[...], wo_ref[d_m:, :])
    h = _rms(x1_ref[...], nw_ref[...]).astype(BF16)
    d_ff = w1_ref.shape[1]
    ffn = None
    for j in range(d_ff // FF_CHUNK):
        cs = slice(j * FF_CHUNK, (j + 1) * FF_CHUNK)
        hid = jnp.square(jnp.maximum(_dot(h, w1_ref[:, cs]), 0.0)).astype(BF16)
        part = _dot(hid, w2_ref[cs, :])
        ffn = part if ffn is None else ffn + part
    acc = x1_ref[...] + ffn
    if final:
        acc = _rms(acc, fw_ref[...])
    o_ref[...] = acc


def _out_ffn(x2, ym, yn, wo, nw, w1, w2, fw, *, final, tm):
    n, d = x2.shape
    d_m, d_n = ym.shape[1], yn.shape[1]
    kern = functools.partial(_out_ffn_kernel, final=final)
    resident = pl.BlockSpec(memory_space=pltpu.VMEM)
    return pl.pallas_call(
        kern,
        grid=(n // tm,),
        in_specs=[
            pl.BlockSpec((tm, d), lambda i: (i, 0)),
            pl.BlockSpec((tm, d_m), lambda i: (i, 0)),
            pl.BlockSpec((tm, d_n), lambda i: (i, 0)),
            resident, resident, resident, resident, resident,
        ],
        out_specs=pl.BlockSpec((tm, d), lambda i: (i, 0)),
        out_shape=jax.ShapeDtypeStruct((n, d), F32),
        scratch_shapes=[pltpu.VMEM((tm, d), F32)],
        compiler_params=pltpu.CompilerParams(
            dimension_semantics=("arbitrary",), vmem_limit_bytes=VMEM_LIMIT),
        name="out_ffn",
    )(x2, ym, yn, wo, nw, w1, w2, fw)


def kernel(x, norm1_w, w_in, conv_w, conv_b, gate_b, mlstm_norm_w, rpb, w_out, norm2_w, w_ff1, w_ff2,
           final_norm_w):
    bsz, seq, d = x.shape
    depth = w_in.shape[0]
    d_m = MLSTM_HEADS * MLSTM_HD
    d_n = NA_HEADS * NA_HD
    n = bsz * seq
    rows = seq // GRID_W
    nch = seq // LCH
    assert w_in.shape[2] == 4 * d_m + N_GATES + 3 * d_n
    assert seq % LCH == 0 and rows % NA_ROWS_PER_STEP == 0 and rows >= NA_KH

    g0 = 4 * d_m
    n0 = g0 + N_GATES
    x2 = x.reshape(n, d)
    for l in range(depth):
        w = w_in[l]
        wp = jnp.concatenate([
            w[:, 0:g0], w[:, n0:n0 + d_n] * (NA_HD ** -0.5 * LOG2E), w[:, n0 + d_n:],
            jnp.pad(w[:, g0:n0], ((0, 0), (0, GATE_PAD - N_GATES)))], axis=1).astype(BF16)
        qk, v, o, qkv_n, gates_t = _in_proj(x2, norm1_w[l][None], wp, d_m=d_m, d_n3=3 * d_n, tm=512)

        y_m = _mlstm(
            gate_b[l].reshape(4, MLSTM_HEADS),
            qk.reshape(bsz, seq, 2 * d_m), v.reshape(bsz, seq, d_m), o.reshape(bsz, seq, d_m),
            gates_t.reshape(4, MLSTM_HEADS, bsz, nch, LCH),
            conv_w[l], conv_b[l][None], mlstm_norm_w[l][None])
        y_n = _natten(qkv_n.reshape(bsz, seq, 3 * d_n), _natten_bias(rpb[l]))

        x2 = _out_ffn(
            x2, y_m.reshape(n, d_m), y_n.reshape(n, d_n), w_out[l].astype(BF16), norm2_w[l][None],
            w_ff1[l].astype(BF16), w_ff2[l].astype(BF16), final_norm_w[None],
            final=(l == depth - 1), tm=512)
    return x2.reshape(bsz, seq, d)
```

```python
import functools

import numpy as np
import jax
import jax.numpy as jnp
from jax import lax
from jax.experimental import pallas as pl
from jax.experimental.pallas import tpu as pltpu

EPS = 1e-6
GRID_W = 64
MLSTM_HEADS = 4
MLSTM_HD = 128
NA_HEADS = 8
NA_HD = 64
NA_KH = 8
NA_KW = 16
CONV_K = 3
N_GATES = 4 * MLSTM_HEADS
LCH = 128
OUT_CHUNKS_PER_ITER = 4
GATE_PAD = 128
GATES_PER_HEAD = 8
GATE_ROWS = MLSTM_HEADS * GATES_PER_HEAD
NEG = -1e30
LOG2E = 1.4426950408889634

BF16 = jnp.bfloat16
F32 = jnp.float32

VMEM_LIMIT = 56 * 1024 * 1024


def _dot(a, b):
    return jnp.dot(a, b, preferred_element_type=F32)


def _dot_nt(a, b):
    return lax.dot_general(a, b, (((1,), (1,)), ((), ())), preferred_element_type=F32)


def _rms(x, w):
    return x * lax.rsqrt(jnp.mean(x * x, axis=-1, keepdims=True) + EPS) * w


def _in_proj_kernel(x_ref, nw_ref, wm_ref, wn_ref, wg_ref, qk_ref, v_ref, o_ref, n_ref, g_ref, *, d_m):
    hn = _rms(x_ref[...], nw_ref[...]).astype(BF16)
    c0 = 2 * d_m
    qk_ref[...] = _dot(hn, wm_ref[:, 0:c0])
    v_ref[...] = _dot(hn, wm_ref[:, c0:c0 + d_m]).astype(BF16)
    o_ref[...] = _dot(hn, wm_ref[:, c0 + d_m:c0 + 2 * d_m]).astype(BF16)
    n_ref[...] = _dot(hn, wn_ref[...]).astype(BF16)
    g = _dot(hn, wg_ref[...])
    for j in range(g_ref.shape[0]):
        g_ref[j] = g[j * LCH:(j + 1) * LCH, :].T[0:GATE_ROWS, :]


def _in_proj(x2, nw, wm, wn, wg, *, layer, d_m, tm):
    n, d = x2.shape
    d_n3 = wn.shape[2]
    kern = functools.partial(_in_proj_kernel, d_m=d_m)
    layer_blk = lambda a: pl.BlockSpec((None,) + a.shape[1:], lambda i: (layer, 0, 0))
    return pl.pallas_call(
        kern,
        grid=(n // tm,),
        in_specs=[
            pl.BlockSpec((tm, d), lambda i: (i, 0)),
            layer_blk(nw), layer_blk(wm), layer_blk(wn), layer_blk(wg),
        ],
        out_specs=[
            pl.BlockSpec((tm, 2 * d_m), lambda i: (i, 0)),
            pl.BlockSpec((tm, d_m), lambda i: (i, 0)),
            pl.BlockSpec((tm, d_m), lambda i: (i, 0)),
            pl.BlockSpec((tm, d_n3), lambda i: (i, 0)),
            pl.BlockSpec((tm // LCH, GATE_ROWS, LCH), lambda i: (i, 0, 0)),
        ],
        out_shape=[
            jax.ShapeDtypeStruct((n, 2 * d_m), F32),
            jax.ShapeDtypeStruct((n, d_m), BF16),
            jax.ShapeDtypeStruct((n, d_m), BF16),
            jax.ShapeDtypeStruct((n, d_n3), BF16),
            jax.ShapeDtypeStruct((n // LCH, GATE_ROWS, LCH), F32),
        ],
        compiler_params=pltpu.CompilerParams(
            dimension_semantics=("arbitrary",), vmem_limit_bytes=VMEM_LIMIT),
        name="in_proj",
    )(x2, nw, wm, wn, wg)


_WAF, _WAB, _GF, _GB, _LFF, _LFB, _TOTF, _MLF, _TOTB, _MLB, _MPF, _MPB = range(12)


def _cumsum_lanes(x):
    lane = lax.broadcasted_iota(jnp.int32, x.shape, 1)
    k = 1
    while k < x.shape[1]:
        x = x + jnp.where(lane >= k, pltpu.roll(x, k, axis=1), 0.0)
        k *= 2
    return x


def _mlstm_kernel(gb_ref, q_ref, k_ref, v_ref, o_ref, g_ref, cwq_ref, cwk_ref, cbq_ref, cbk_ref,
                  nw_ref, y_ref, q32s, ks, kt, vaug, sf, sb, stf, stb, gs, *, nch, layer):
    hd = pl.program_id(1)
    seq = nch * LCH
    dh = MLSTM_HD

    i_f = (g_ref[0, :, 0, :] + gb_ref[layer, 0, hd]) * LOG2E
    f_f = g_ref[0, :, 1, :] + gb_ref[layer, 1, hd]
    i_b = (g_ref[0, :, 2, :] + gb_ref[layer, 2, hd]) * LOG2E
    f_b = g_ref[0, :, 3, :] + gb_ref[layer, 3, hd]

    def logsig(f):
        return jnp.minimum(f, 0.0) - jnp.log1p(jnp.exp(-jnp.abs(f)))

    lf_f = logsig(f_f) * LOG2E
    lf_b = logsig(f_b) * LOG2E
    tot_f = jnp.sum(lf_f, axis=1, keepdims=True)
    tot_b = jnp.sum(lf_b, axis=1, keepdims=True)
    b_f = _cumsum_lanes(lf_f)
    b_b = tot_b - _cumsum_lanes(lf_b) + lf_b
    a_f = tot_f - b_f + i_f
    a_b = tot_b - b_b + i_b
    ml_f = jnp.max(a_f, axis=1, keepdims=True)
    ml_b = jnp.max(a_b, axis=1, keepdims=True)
    shp = (nch, LCH)
    gs[_WAF] = jnp.exp2(a_f - ml_f)
    gs[_WAB] = jnp.exp2(a_b - ml_b)
    gs[_GF] = i_f - b_f
    gs[_GB] = i_b - b_b
    gs[_LFF] = lf_f
    gs[_LFB] = lf_b
    gs[_TOTF] = jnp.broadcast_to(tot_f, shp)
    gs[_MLF] = jnp.broadcast_to(ml_f, shp)
    gs[_TOTB] = jnp.broadcast_to(tot_b, shp)
    gs[_MLB] = jnp.broadcast_to(ml_b, shp)

    row = lax.broadcasted_iota(jnp.int32, (LCH, dh), 0)
    ones = jnp.ones((LCH, dh), BF16)

    def conv_act(ref, cw_ref, cb_ref, c):
        r0 = pl.multiple_of(c * LCH, LCH)
        x = ref[0, pl.ds(r0, LCH), :]
        prev = ref[0, pl.ds(jnp.maximum(r0 - 1, 0), 1), :] * jnp.where(c > 0, 1.0, 0.0)
        nxt = ref[0, pl.ds(jnp.minimum(r0 + LCH, seq - 1), 1), :] * jnp.where(c < nch - 1, 1.0, 0.0)
        xm1 = jnp.where(row == 0, prev, pltpu.roll(x, 1, axis=0))
        xp1 = jnp.where(row == LCH - 1, nxt, pltpu.roll(x, LCH - 1, axis=0))
        y = cw_ref[0:1, :] * xm1 + cw_ref[1:2, :] * x + cw_ref[2:3, :] * xp1 + cb_ref[...]
        return y * jax.nn.sigmoid(y)

    def prep_body(c, carry):
        r0 = pl.multiple_of(c * LCH, LCH)
        q32s[pl.ds(r0, LCH), :] = conv_act(q_ref, cwq_ref, cbq_ref, c)
        ka = conv_act(k_ref, cwk_ref, cbk_ref, c) * (dh ** -0.5)
        ks[pl.ds(r0, LCH), :] = ka.astype(BF16)
        kt[c] = ka.T
        vaug[pl.ds(r0, LCH), 0:dh] = v_ref[0, pl.ds(r0, LCH), :]
        vaug[pl.ds(r0, LCH), dh:2 * dh] = ones
        return carry

    lax.fori_loop(0, nch, prep_body, 0, unroll=2)

    stf[...] = jnp.zeros_like(stf)
    stb[...] = jnp.zeros_like(stb)

    def scan_dir(c, m_prev, st, s_out, wa_i, tot_i, ml_i, mp_i):
        r0 = pl.multiple_of(c * LCH, LCH)
        tot = gs[tot_i, pl.ds(c, 1), :]
        ml = gs[ml_i, pl.ds(c, 1), :]
        gs[mp_i, pl.ds(c, 1), :] = m_prev
        s_prev = st[...]
        s_out[c] = s_prev.astype(BF16)
        a = (kt[c] * gs[wa_i, pl.ds(c, 1), :]).astype(BF16)
        u = _dot(a, vaug[pl.ds(r0, LCH), :])
        m_new = jnp.maximum(tot + m_prev, ml)
        s_old = jnp.exp2(tot + m_prev - m_new)
        s_loc = jnp.exp2(ml - m_new)
        s_old2 = jnp.concatenate([s_old, s_old], axis=1)
        s_loc2 = jnp.concatenate([s_loc, s_loc], axis=1)
        st[...] = s_old2 * s_prev + s_loc2 * u
        return m_new

    def scan_body(i, carry):
        m_f, m_b = carry
        m_f = scan_dir(i, m_f, stf, sf, _WAF, _TOTF, _MLF, _MPF)
        m_b = scan_dir(nch - 1 - i, m_b, stb, sb, _WAB, _TOTB, _MLB, _MPB)
        return m_f, m_b

    zero_row = jnp.zeros((1, LCH), F32)
    lax.fori_loop(0, nch, scan_body, (zero_row, zero_row), unroll=4)

    jj = lax.broadcasted_iota(jnp.int32, (LCH, LCH), 0)
    ss = lax.broadcasted_iota(jnp.int32, (LCH, LCH), 1)
    mask_f = ss <= jj
    mask_b = ss >= jj

    dirs = ((mask_f, sf, _GF, _LFF, _MPF), (mask_b, sb, _GB, _LFB, _MPB))

    def weights_dir(c, q32, qk, mask, g_i, lf_i, mp_i):
        g_row = gs[g_i, pl.ds(c, 1), :]
        lf_row = gs[lf_i, pl.ds(c, 1), :]
        m_prev = gs[mp_i, pl.ds(c, 1), :]
        gm = jnp.where(mask, g_row, -jnp.inf)
        mj = jnp.maximum(jnp.max(gm, axis=1, keepdims=True), m_prev)
        bj = jnp.sum(jnp.where(mask, lf_row, 0.0), axis=1, keepdims=True)
        p = (jnp.exp2(gm - mj) * qk).astype(BF16)
        wq = (jnp.exp2(m_prev - mj) * q32).astype(BF16)
        return jnp.concatenate([wq, p], axis=1), jnp.exp2(-bj - mj)

    def out_body(it, carry):
        cs = [it * OUT_CHUNKS_PER_ITER + j for j in range(OUT_CHUNKS_PER_ITER)]
        r0s = [pl.multiple_of(c * LCH, LCH) for c in cs]
        q32 = [q32s[pl.ds(r0, LCH), :] for r0 in r0s]
        qk = [_dot_nt(q.astype(BF16), ks[pl.ds(r0, LCH), :]) for q, r0 in zip(q32, r0s)]
        wts = [[weights_dir(c, q, s, mask, g_i, lf_i, mp_i) for (mask, _, g_i, lf_i, mp_i) in dirs]
               for c, q, s in zip(cs, q32, qk)]
        hs = []
        for c, r0, wt in zip(cs, r0s, wts):
            vc = vaug[pl.ds(r0, LCH), :]
            h = None
            for (lhs, bound), (_, s_ref, _, _, _) in zip(wt, dirs):
                tot = _dot(lhs, jnp.concatenate([s_ref[c], vc], axis=0))
                hd_ = tot[:, 0:dh] / jnp.maximum(jnp.abs(tot[:, dh:2 * dh]), bound)
                h = hd_ if h is None else h + hd_
            hs.append(h)
        for r0, h in zip(r0s, hs):
            hn = h * lax.rsqrt(jnp.mean(h * h, axis=-1, keepdims=True) + EPS) * nw_ref[...]
            o = o_ref[0, pl.ds(r0, LCH), :].astype(F32)
            y_ref[0, pl.ds(r0, LCH), :] = (jax.nn.sigmoid(o) * hn).astype(BF16)
        return carry

    lax.fori_loop(0, nch // OUT_CHUNKS_PER_ITER, out_body, 0)


def _mlstm(gate_b, qk, v, o, gates, conv_w, conv_b, norm_w, *, layer):
    bsz, seq, d_m = v.shape
    nh, dh = MLSTM_HEADS, MLSTM_HD
    nch = seq // LCH
    kern = functools.partial(_mlstm_kernel, nch=nch, layer=layer)
    seq_blk = lambda off: pl.BlockSpec((1, seq, dh), lambda b, h: (b, 0, h + off))
    vec_blk = lambda rows, off: pl.BlockSpec((None, rows, dh), lambda b, h: (layer, 0, h + off))
    return pl.pallas_call(
        kern,
        grid=(bsz, nh),
        in_specs=[
            pl.BlockSpec(memory_space=pltpu.SMEM),
            seq_blk(0), seq_blk(nh),
            seq_blk(0), seq_blk(0),
            pl.BlockSpec((1, nch, GATES_PER_HEAD, LCH), lambda b, h: (b, 0, h, 0)),
            vec_blk(CONV_K, 0), vec_blk(CONV_K, nh),
            vec_blk(1, 0), vec_blk(1, nh),
            vec_blk(1, 0),
        ],
        out_specs=pl.BlockSpec((1, seq, dh), lambda b, h: (b, 0, h)),
        out_shape=jax.ShapeDtypeStruct((bsz, seq, d_m), BF16),
        scratch_shapes=[
            pltpu.VMEM((seq, dh), F32),
            pltpu.VMEM((seq, dh), BF16),
            pltpu.VMEM((nch, dh, LCH), F32),
            pltpu.VMEM((seq, 2 * dh), BF16),
            pltpu.VMEM((nch, dh, 2 * dh), BF16),
            pltpu.VMEM((nch, dh, 2 * dh), BF16),
            pltpu.VMEM((dh, 2 * dh), F32),
            pltpu.VMEM((dh, 2 * dh), F32),
            pltpu.VMEM((12, nch, LCH), F32),
        ],
        compiler_params=pltpu.CompilerParams(
            dimension_semantics=("arbitrary", "arbitrary"), vmem_limit_bytes=VMEM_LIMIT),
        name="mlstm",
    )(gate_b, qk, qk, v, o, gates, conv_w, conv_w, conv_b, conv_b, norm_w)


NA_ROWS_PER_STEP = 8
NA_ROWS_PER_ITER = 2


def _natten_kernel(q_ref, k_ref, v_ref, th_ref, o_ref, bias_ref, s_scr, e_scr, l_scr, *, rows):
    rb = pl.program_id(1)
    band = NA_KH * GRID_W
    npair = NA_HEADS // 2
    lane = lax.broadcasted_iota(jnp.int32, (GRID_W, 2 * NA_HD), 1)
    lo = lane < NA_HD
    units = [(j, p) for j in range(NA_ROWS_PER_ITER) for p in range(npair)]

    @pl.when((pl.program_id(0) == 0) & (rb == 0))
    def _():
        for h in range(NA_HEADS):
            for d in range(NA_KH):
                first = NA_KH - 1 - d
                par = first % 2
                off = (first - par) * GRID_W
                bias_ref[h // 2, d, (h % 2) * GRID_W:(h % 2 + 1) * GRID_W, :] = (
                    th_ref[par, h, :, off:off + band])

    def rows_body(it, carry):
        q0s, k0s, ds = [], [], []
        for j in range(NA_ROWS_PER_ITER):
            i = it * NA_ROWS_PER_ITER + j
            r = rb * NA_ROWS_PER_STEP + i
            rs = jnp.clip(r - NA_KH // 2, 0, rows - NA_KH)
            ds.append(r - rs)
            q0s.append(pl.multiple_of(i * GRID_W, GRID_W))
            k0s.append(pl.multiple_of(rs * GRID_W, GRID_W))
        for u, (j, p) in enumerate(units):
            cs = slice(p * 2 * NA_HD, (p + 1) * 2 * NA_HD)
            qp = q_ref[0, pl.ds(q0s[j], GRID_W), cs].astype(F32)
            qq = jnp.concatenate([jnp.where(lo, qp, 0.0), jnp.where(lo, 0.0, qp)], axis=0).astype(BF16)
            s_scr[u] = _dot_nt(qq, k_ref[0, pl.ds(k0s[j], band), cs]) + bias_ref[p, ds[j]]
        for u in range(len(units)):
            s = s_scr[u]
            e = jnp.exp2(s - jnp.max(s, axis=-1, keepdims=True))
            l_scr[u] = jnp.broadcast_to(jnp.sum(e, axis=-1, keepdims=True), l_scr.shape[1:])
            e_scr[u] = e.astype(BF16)
        for u, (j, p) in enumerate(units):
            cs = slice(p * 2 * NA_HD, (p + 1) * 2 * NA_HD)
            o2 = _dot(e_scr[u], v_ref[0, pl.ds(k0s[j], band), cs]) / l_scr[u]
            op = jnp.where(lo, o2[0:GRID_W], o2[GRID_W:2 * GRID_W])
            o_ref[0, pl.ds(q0s[j], GRID_W), cs] = op.astype(BF16)
        return carry

    lax.fori_loop(0, NA_ROWS_PER_STEP // NA_ROWS_PER_ITER, rows_body, 0)


def _natten(qkv, th, *, layer):
    bsz, seq, d3 = qkv.shape
    d_n = d3 // 3
    rows = seq // GRID_W
    tq = NA_ROWS_PER_STEP * GRID_W
    n_units = NA_ROWS_PER_ITER * NA_HEADS // 2
    kern = functools.partial(_natten_kernel, rows=rows)
    return pl.pallas_call(
        kern,
        grid=(bsz, rows // NA_ROWS_PER_STEP),
        in_specs=[
            pl.BlockSpec((1, tq, d_n), lambda b, r: (b, r, 0)),
            pl.BlockSpec((1, seq, d_n), lambda b, r: (b, 0, 1)),
            pl.BlockSpec((1, seq, d_n), lambda b, r: (b, 0, 2)),
            pl.BlockSpec((None,) + th.shape[1:], lambda b, r: (layer, 0, 0, 0, 0)),
        ],
        out_specs=pl.BlockSpec((1, tq, d_n), lambda b, r: (b, r, 0)),
        out_shape=jax.ShapeDtypeStruct((bsz, seq, d_n), BF16),
        scratch_shapes=[
            pltpu.VMEM((NA_HEADS // 2, NA_KH, 2 * GRID_W, NA_KH * GRID_W), F32),
            pltpu.VMEM((n_units, 2 * GRID_W, NA_KH * GRID_W), F32),
            pltpu.VMEM((n_units, 2 * GRID_W, NA_KH * GRID_W), BF16),
            pltpu.VMEM((n_units, 2 * GRID_W, 2 * NA_HD), F32),
        ],
        compiler_params=pltpu.CompilerParams(
            dimension_semantics=("arbitrary", "arbitrary"), vmem_limit_bytes=VMEM_LIMIT),
        name="natten",
    )(qkv, qkv, qkv, th)


def _natten_bias_tables(rpb):
    depth, nh = rpb.shape[0], rpb.shape[1]
    n_rr, n_rc = 2 * NA_KH - 1, 2 * NA_KW - 1
    cols = np.arange(GRID_W)
    c_start = np.clip(cols - NA_KW // 2, 0, GRID_W - NA_KW)
    kc = np.arange(GRID_W)
    valid = (kc[None, :] >= c_start[:, None]) & (kc[None, :] < c_start[:, None] + NA_KW)
    rel_c = kc[None, :] - cols[:, None] + NA_KW - 1
    onehot = ((rel_c[None] == np.arange(n_rc)[:, None, None]) & valid[None]).astype(np.float32)
    t = jnp.dot(rpb.astype(F32).reshape(depth * nh * n_rr, n_rc), onehot.reshape(n_rc, GRID_W * GRID_W),
                precision=lax.Precision.HIGHEST).reshape(depth, nh, n_rr, GRID_W, GRID_W)
    t = t * LOG2E + np.where(valid, 0.0, NEG).astype(np.float32)
    t = t.transpose(0, 1, 3, 2, 4)
    t0 = t.reshape(depth, nh, GRID_W, n_rr * GRID_W)
    t1 = jnp.pad(t0[..., GRID_W:], ((0, 0), (0, 0), (0, 0), (0, GRID_W)))
    return jnp.stack([t0, t1], axis=1)


FF_CHUNK = 1024


def _out_ffn_kernel(x_ref, ym_ref, yn_ref, wo_ref, nw_ref, w1_ref, w2_ref, fw_ref, o_ref, x1_ref, *,
                    final):
    d_m = ym_ref.shape[1]
    x1_ref[...] = x_ref[...] + _dot(ym_ref[...], wo_ref[0:d_m, :]) + _dot(yn_ref[...], wo_ref[d_m:, :])
    h = _rms(x1_ref[...], nw_ref[...]).astype(BF16)
    d_ff = w1_ref.shape[1]
    ffn = None
    for j in range(d_ff // FF_CHUNK):
        cs = slice(j * FF_CHUNK, (j + 1) * FF_CHUNK)
        hid = jnp.square(jnp.maximum(_dot(h, w1_ref[:, cs]), 0.0)).astype(BF16)
        part = _dot(hid, w2_ref[cs, :])
        ffn = part if ffn is None else ffn + part
    acc = x1_ref[...] + ffn
    if final:
        acc = _rms(acc, fw_ref[...])
    o_ref[...] = acc


def _out_ffn(x2, ym, yn, wo, nw, w1, w2, fw, *, layer, final, tm):
    n, d = x2.shape
    d_m, d_n = ym.shape[1], yn.shape[1]
    kern = functools.partial(_out_ffn_kernel, final=final)
    layer_blk = lambda a: pl.BlockSpec((None,) + a.shape[1:], lambda i: (layer, 0, 0),
                                       pipeline_mode=pl.Buffered(1))
    return pl.pallas_call(
        kern,
        grid=(n // tm,),
        in_specs=[
            pl.BlockSpec((tm, d), lambda i: (i, 0)),
            pl.BlockSpec((tm, d_m), lambda i: (i, 0)),
            pl.BlockSpec((tm, d_n), lambda i: (i, 0)),
            layer_blk(wo), layer_blk(nw), layer_blk(w1), layer_blk(w2),
            pl.BlockSpec(memory_space=pltpu.VMEM),
        ],
        out_specs=pl.BlockSpec((tm, d), lambda i: (i, 0)),
        out_shape=jax.ShapeDtypeStruct((n, d), F32),
        scratch_shapes=[pltpu.VMEM((tm, d), F32)],
        compiler_params=pltpu.CompilerParams(
            dimension_semantics=("arbitrary",), vmem_limit_bytes=VMEM_LIMIT),
        name="out_ffn",
    )(x2, ym, yn, wo, nw, w1, w2, fw)


def kernel(x, norm1_w, w_in, conv_w, conv_b, gate_b, mlstm_norm_w, rpb, w_out, norm2_w, w_ff1, w_ff2,
           final_norm_w):
    bsz, seq, d = x.shape
    depth = w_in.shape[0]
    d_m = MLSTM_HEADS * MLSTM_HD
    d_n = NA_HEADS * NA_HD
    n = bsz * seq
    rows = seq // GRID_W
    nch = seq // LCH
    assert w_in.shape[2] == 4 * d_m + N_GATES + 3 * d_n
    assert seq % LCH == 0 and rows % NA_ROWS_PER_STEP == 0 and rows >= NA_KH

    g0 = 4 * d_m
    n0 = g0 + N_GATES
    w_m = w_in[:, :, 0:g0].astype(BF16)
    w_n = jnp.concatenate([w_in[:, :, n0:n0 + d_n] * (NA_HD ** -0.5 * LOG2E),
                           w_in[:, :, n0 + d_n:]], axis=2).astype(BF16)
    w_g = w_in[:, :, g0:n0].reshape(depth, d, 4, MLSTM_HEADS).transpose(0, 1, 3, 2)
    w_g = jnp.pad(w_g, ((0, 0), (0, 0), (0, 0), (0, GATES_PER_HEAD - 4))).reshape(depth, d, GATE_ROWS)
    w_g = jnp.pad(w_g, ((0, 0), (0, 0), (0, GATE_PAD - GATE_ROWS))).astype(BF16)
    w_o, w_1, w_2 = w_out.astype(BF16), w_ff1.astype(BF16), w_ff2.astype(BF16)
    n1, n2 = norm1_w[:, None, :], norm2_w[:, None, :]
    gb = gate_b.reshape(depth, 4, MLSTM_HEADS)
    cb, mn = conv_b[:, None, :], mlstm_norm_w[:, None, :]
    th = _natten_bias_tables(rpb)

    x2 = x.reshape(n, d)
    for l in range(depth):
        qk, v, o, qkv_n, gates = _in_proj(x2, n1, w_m, w_n, w_g, layer=l, d_m=d_m, tm=512)
        y_m = _mlstm(
            gb, qk.reshape(bsz, seq, 2 * d_m), v.reshape(bsz, seq, d_m), o.reshape(bsz, seq, d_m),
            gates.reshape(bsz, nch, GATE_ROWS, LCH), conv_w, cb, mn, layer=l)
        y_n = _natten(qkv_n.reshape(bsz, seq, 3 * d_n), th, layer=l)
        x2 = _out_ffn(
            x2, y_m.reshape(n, d_m), y_n.reshape(n, d_n), w_o, n2, w_1, w_2, final_norm_w[None],
            layer=l, final=(l == depth - 1), tm=512)
    return x2.reshape(bsz, seq, d)
```

```python
import functools

import numpy as np
import jax
import jax.numpy as jnp
from jax import lax
from jax.experimental import pallas as pl
from jax.experimental.pallas import tpu as pltpu

EPS = 1e-6
GRID_W = 64
MLSTM_HEADS = 4
MLSTM_HD = 128
NA_HEADS = 8
NA_HD = 64
NA_KH = 8
NA_KW = 16
CONV_K = 3
N_GATES = 4 * MLSTM_HEADS
LCH = 128
OUT_CHUNKS_PER_ITER = 4
GATE_PAD = 128
GATES_PER_HEAD = 8
GATE_ROWS = MLSTM_HEADS * GATES_PER_HEAD
NEG = -1e30
LOG2E = 1.4426950408889634

BF16 = jnp.bfloat16
F32 = jnp.float32

VMEM_LIMIT = 56 * 1024 * 1024


def _dot(a, b):
    return jnp.dot(a, b, preferred_element_type=F32)


def _dot_nt(a, b):
    return lax.dot_general(a, b, (((1,), (1,)), ((), ())), preferred_element_type=F32)


def _rms(x, w):
    return x * lax.rsqrt(jnp.mean(x * x, axis=-1, keepdims=True) + EPS) * w


def _in_proj_kernel(x_ref, xp_ref, xn_ref, nw_ref, wm_ref, wn_ref, wg_ref, cw_ref, cb_ref,
                    q_ref, k_ref, kt_ref, va_ref, o_ref, n_ref, g_ref, *, d_m, tiles_per_seq):
    i = pl.program_id(0)
    tm = x_ref.shape[0]
    dh = MLSTM_HD
    nw = nw_ref[...]
    hn = jnp.concatenate(
        [_rms(x_ref[...], nw), _rms(xp_ref[...], nw), _rms(xn_ref[...], nw)], axis=0).astype(BF16)
    c0 = 2 * d_m
    first = jnp.where(i % tiles_per_seq == 0, 0.0, 1.0)
    last = jnp.where(i % tiles_per_seq == tiles_per_seq - 1, 0.0, 1.0)
    wb = 2 * dh
    row = lax.broadcasted_iota(jnp.int32, (tm, wb), 0)

    def conv_act(pre, c0_):
        cs = slice(c0_, c0_ + wb)
        x = pre[0:tm]
        xm1 = jnp.where(row == 0, pre[tm + HALO - 1:tm + HALO] * first, pltpu.roll(x, 1, axis=0))
        xp1 = jnp.where(row == tm - 1, pre[tm + HALO:tm + HALO + 1] * last, pltpu.roll(x, tm - 1, axis=0))
        y = cw_ref[0:1, cs] * xm1 + cw_ref[1:2, cs] * x + cw_ref[2:3, cs] * xp1 + cb_ref[:, cs]
        return y * jax.nn.sigmoid(y)

    hn_t = hn[0:tm]
    ones = jnp.ones((tm, dh), BF16)
    for hp in range(MLSTM_HEADS // 2):
        cs = slice(hp * wb, (hp + 1) * wb)
        q_ref[:, cs] = conv_act(_dot(hn, wm_ref[:, cs]), hp * wb)
        ka = conv_act(_dot(hn, wm_ref[:, d_m + hp * wb:d_m + (hp + 1) * wb]), d_m + hp * wb) * (dh ** -0.5)
        k_ref[:, cs] = ka.astype(BF16)
        v = _dot(hn_t, wm_ref[:, c0 + hp * wb:c0 + (hp + 1) * wb]).astype(BF16)
        for hh in range(2):
            h = 2 * hp + hh
            for j in range(tm // LCH):
                kt_ref[h, j] = ka[j * LCH:(j + 1) * LCH, hh * dh:(hh + 1) * dh].T
            va_ref[:, 2 * h * dh:(2 * h + 1) * dh] = v[:, hh * dh:(hh + 1) * dh]
            va_ref[:, (2 * h + 1) * dh:(2 * h + 2) * dh] = ones
    o_ref[...] = _dot(hn_t, wm_ref[:, c0 + d_m:c0 + 2 * d_m]).astype(BF16)
    n_ref[...] = _dot(hn_t, wn_ref[...]).astype(BF16)
    g = _dot(hn_t, wg_ref[...])
    for j in range(g_ref.shape[0]):
        g_ref[j] = g[j * LCH:(j + 1) * LCH, :].T[0:GATE_ROWS, :]


HALO = 8


def _in_proj(x2, nw, wm, wn, wg, cw, cb, *, layer, seq, d_m, tm):
    n, d = x2.shape
    d_n3 = wn.shape[2]
    nh, dh = MLSTM_HEADS, MLSTM_HD
    assert seq % tm == 0 and tm % LCH == 0
    kern = functools.partial(_in_proj_kernel, d_m=d_m, tiles_per_seq=seq // tm)
    layer_blk = lambda a: pl.BlockSpec((None,) + a.shape[1:], lambda i: (layer, 0, 0))
    hb = tm // HALO
    return pl.pallas_call(
        kern,
        grid=(n // tm,),
        in_specs=[
            pl.BlockSpec((tm, d), lambda i: (i, 0)),
            pl.BlockSpec((HALO, d), lambda i: (jnp.maximum(i * hb - 1, 0), 0)),
            pl.BlockSpec((HALO, d), lambda i: (jnp.minimum((i + 1) * hb, n // HALO - 1), 0)),
            layer_blk(nw), layer_blk(wm), layer_blk(wn), layer_blk(wg), layer_blk(cw), layer_blk(cb),
        ],
        out_specs=[
            pl.BlockSpec((tm, d_m), lambda i: (i, 0)),
            pl.BlockSpec((tm, d_m), lambda i: (i, 0)),
            pl.BlockSpec((nh, tm // LCH, dh, LCH), lambda i: (0, i, 0, 0)),
            pl.BlockSpec((tm, 2 * d_m), lambda i: (i, 0)),
            pl.BlockSpec((tm, d_m), lambda i: (i, 0)),
            pl.BlockSpec((tm, d_n3), lambda i: (i, 0)),
            pl.BlockSpec((tm // LCH, GATE_ROWS, LCH), lambda i: (i, 0, 0)),
        ],
        out_shape=[
            jax.ShapeDtypeStruct((n, d_m), F32),
            jax.ShapeDtypeStruct((n, d_m), BF16),
            jax.ShapeDtypeStruct((nh, n // LCH, dh, LCH), F32),
            jax.ShapeDtypeStruct((n, 2 * d_m), BF16),
            jax.ShapeDtypeStruct((n, d_m), BF16),
            jax.ShapeDtypeStruct((n, d_n3), BF16),
            jax.ShapeDtypeStruct((n // LCH, GATE_ROWS, LCH), F32),
        ],
        compiler_params=pltpu.CompilerParams(
            dimension_semantics=("arbitrary",), vmem_limit_bytes=VMEM_LIMIT),
        name="in_proj",
    )(x2, x2, x2, nw, wm, wn, wg, cw, cb)


_WAF, _WAB, _GF, _GB, _LFF, _LFB, _TOTF, _MLF, _TOTB, _MLB, _MPF, _MPB = range(12)


def _cumsum_lanes(x):
    lane = lax.broadcasted_iota(jnp.int32, x.shape, 1)
    k = 1
    while k < x.shape[1]:
        x = x + jnp.where(lane >= k, pltpu.roll(x, k, axis=1), 0.0)
        k *= 2
    return x


def _mlstm_kernel(gb_ref, q_ref, k_ref, kt_ref, va_ref, o_ref, g_ref, nw_ref, y_ref,
                  sf, sb, stf, stb, gs, *, nch, layer):
    hd = pl.program_id(1)
    dh = MLSTM_HD

    i_f = (g_ref[0, :, 0, :] + gb_ref[layer, 0, hd]) * LOG2E
    f_f = g_ref[0, :, 1, :] + gb_ref[layer, 1, hd]
    i_b = (g_ref[0, :, 2, :] + gb_ref[layer, 2, hd]) * LOG2E
    f_b = g_ref[0, :, 3, :] + gb_ref[layer, 3, hd]

    def logsig(f):
        return jnp.minimum(f, 0.0) - jnp.log1p(jnp.exp(-jnp.abs(f)))

    lf_f = logsig(f_f) * LOG2E
    lf_b = logsig(f_b) * LOG2E
    tot_f = jnp.sum(lf_f, axis=1, keepdims=True)
    tot_b = jnp.sum(lf_b, axis=1, keepdims=True)
    b_f = _cumsum_lanes(lf_f)
    b_b = tot_b - _cumsum_lanes(lf_b) + lf_b
    a_f = tot_f - b_f + i_f
    a_b = tot_b - b_b + i_b
    ml_f = jnp.max(a_f, axis=1, keepdims=True)
    ml_b = jnp.max(a_b, axis=1, keepdims=True)
    shp = (nch, LCH)
    gs[_WAF] = jnp.exp2(a_f - ml_f)
    gs[_WAB] = jnp.exp2(a_b - ml_b)
    gs[_GF] = i_f - b_f
    gs[_GB] = i_b - b_b
    gs[_LFF] = lf_f
    gs[_LFB] = lf_b
    gs[_TOTF] = jnp.broadcast_to(tot_f, shp)
    gs[_MLF] = jnp.broadcast_to(ml_f, shp)
    gs[_TOTB] = jnp.broadcast_to(tot_b, shp)
    gs[_MLB] = jnp.broadcast_to(ml_b, shp)

    stf[...] = jnp.zeros_like(stf)
    stb[...] = jnp.zeros_like(stb)

    def scan_dir(c, m_prev, st, s_out, wa_i, tot_i, ml_i, mp_i):
        r0 = pl.multiple_of(c * LCH, LCH)
        tot = gs[tot_i, pl.ds(c, 1), :]
        ml = gs[ml_i, pl.ds(c, 1), :]
        gs[mp_i, pl.ds(c, 1), :] = m_prev
        s_prev = st[...]
        s_out[c] = s_prev.astype(BF16)
        a = (kt_ref[0, 0, c] * gs[wa_i, pl.ds(c, 1), :]).astype(BF16)
        u = _dot(a, va_ref[0, pl.ds(r0, LCH), :])
        m_new = jnp.maximum(tot + m_prev, ml)
        s_old = jnp.exp2(tot + m_prev - m_new)
        s_loc = jnp.exp2(ml - m_new)
        s_old2 = jnp.concatenate([s_old, s_old], axis=1)
        s_loc2 = jnp.concatenate([s_loc, s_loc], axis=1)
        st[...] = s_old2 * s_prev + s_loc2 * u
        return m_new

    def scan_body(i, carry):
        m_f, m_b = carry
        m_f = scan_dir(i, m_f, stf, sf, _WAF, _TOTF, _MLF, _MPF)
        m_b = scan_dir(nch - 1 - i, m_b, stb, sb, _WAB, _TOTB, _MLB, _MPB)
        return m_f, m_b

    zero_row = jnp.zeros((1, LCH), F32)
    lax.fori_loop(0, nch, scan_body, (zero_row, zero_row), unroll=4)

    jj = lax.broadcasted_iota(jnp.int32, (LCH, LCH), 0)
    ss = lax.broadcasted_iota(jnp.int32, (LCH, LCH), 1)
    mask_f = ss <= jj
    mask_b = ss >= jj

    dirs = ((mask_f, sf, _GF, _LFF, _MPF), (mask_b, sb, _GB, _LFB, _MPB))

    def weights_dir(c, q32, qk, mask, g_i, lf_i, mp_i):
        g_row = gs[g_i, pl.ds(c, 1), :]
        lf_row = gs[lf_i, pl.ds(c, 1), :]
        m_prev = gs[mp_i, pl.ds(c, 1), :]
        gm = jnp.where(mask, g_row, -jnp.inf)
        mj = jnp.maximum(jnp.max(gm, axis=1, keepdims=True), m_prev)
        bj = jnp.sum(jnp.where(mask, lf_row, 0.0), axis=1, keepdims=True)
        p = (jnp.exp2(gm - mj) * qk).astype(BF16)
        wq = (jnp.exp2(m_prev - mj) * q32).astype(BF16)
        return jnp.concatenate([wq, p], axis=1), jnp.exp2(-bj - mj)

    def out_body(it, carry):
        cs = [it * OUT_CHUNKS_PER_ITER + j for j in range(OUT_CHUNKS_PER_ITER)]
        r0s = [pl.multiple_of(c * LCH, LCH) for c in cs]
        q32 = [q_ref[0, pl.ds(r0, LCH), :] for r0 in r0s]
        qk = [_dot_nt(q.astype(BF16), k_ref[0, pl.ds(r0, LCH), :]) for q, r0 in zip(q32, r0s)]
        wts = [[weights_dir(c, q, s, mask, g_i, lf_i, mp_i) for (mask, _, g_i, lf_i, mp_i) in dirs]
               for c, q, s in zip(cs, q32, qk)]
        hs = []
        for c, r0, wt in zip(cs, r0s, wts):
            vc = va_ref[0, pl.ds(r0, LCH), :]
            h = None
            for (lhs, bound), (_, s_ref, _, _, _) in zip(wt, dirs):
                tot = _dot(lhs, jnp.concatenate([s_ref[c], vc], axis=0))
                hd_ = tot[:, 0:dh] / jnp.maximum(jnp.abs(tot[:, dh:2 * dh]), bound)
                h = hd_ if h is None else h + hd_
            hs.append(h)
        for r0, h in zip(r0s, hs):
            hn = h * lax.rsqrt(jnp.mean(h * h, axis=-1, keepdims=True) + EPS) * nw_ref[...]
            o = o_ref[0, pl.ds(r0, LCH), :].astype(F32)
            y_ref[0, pl.ds(r0, LCH), :] = (jax.nn.sigmoid(o) * hn).astype(BF16)
        return carry

    lax.fori_loop(0, nch // OUT_CHUNKS_PER_ITER, out_body, 0)


def _mlstm(gate_b, q, k, kt, va, o, gates, norm_w, *, layer):
    bsz, seq, d_m = q.shape
    nh, dh = MLSTM_HEADS, MLSTM_HD
    nch = seq // LCH
    kern = functools.partial(_mlstm_kernel, nch=nch, layer=layer)
    seq_blk = lambda width: pl.BlockSpec((1, seq, width), lambda b, h: (b, 0, h))
    return pl.pallas_call(
        kern,
        grid=(bsz, nh),
        in_specs=[
            pl.BlockSpec(memory_space=pltpu.SMEM),
            seq_blk(dh), seq_blk(dh),
            pl.BlockSpec((1, 1, nch, dh, LCH), lambda b, h: (h, b, 0, 0, 0)),
            seq_blk(2 * dh), seq_blk(dh),
            pl.BlockSpec((1, nch, GATES_PER_HEAD, LCH), lambda b, h: (b, 0, h, 0)),
            pl.BlockSpec((None, 1, dh), lambda b, h: (layer, 0, h)),
        ],
        out_specs=pl.BlockSpec((1, seq, dh), lambda b, h: (b, 0, h)),
        out_shape=jax.ShapeDtypeStruct((bsz, seq, d_m), BF16),
        scratch_shapes=[
            pltpu.VMEM((nch, dh, 2 * dh), BF16),
            pltpu.VMEM((nch, dh, 2 * dh), BF16),
            pltpu.VMEM((dh, 2 * dh), F32),
            pltpu.VMEM((dh, 2 * dh), F32),
            pltpu.VMEM((12, nch, LCH), F32),
        ],
        compiler_params=pltpu.CompilerParams(
            dimension_semantics=("arbitrary", "arbitrary"), vmem_limit_bytes=VMEM_LIMIT),
        name="mlstm",
    )(gate_b, q, k, kt, va, o, gates, norm_w)


NA_ROWS_PER_STEP = 8
NA_ROWS_PER_ITER = 2


def _natten_kernel(q_ref, k_ref, v_ref, th_ref, o_ref, bias_ref, s_scr, e_scr, l_scr, *, rows):
    rb = pl.program_id(1)
    band = NA_KH * GRID_W
    npair = NA_HEADS // 2
    lane = lax.broadcasted_iota(jnp.int32, (GRID_W, 2 * NA_HD), 1)
    lo = lane < NA_HD
    units = [(j, p) for j in range(NA_ROWS_PER_ITER) for p in range(npair)]

    @pl.when((pl.program_id(0) == 0) & (rb == 0))
    def _():
        for h in range(NA_HEADS):
            for d in range(NA_KH):
                first = NA_KH - 1 - d
                par = first % 2
                off = (first - par) * GRID_W
                bias_ref[h // 2, d, (h % 2) * GRID_W:(h % 2 + 1) * GRID_W, :] = (
                    th_ref[par, h, :, off:off + band])

    def rows_body(it, carry):
        q0s, k0s, ds = [], [], []
        for j in range(NA_ROWS_PER_ITER):
            i = it * NA_ROWS_PER_ITER + j
            r = rb * NA_ROWS_PER_STEP + i
            rs = jnp.clip(r - NA_KH // 2, 0, rows - NA_KH)
            ds.append(r - rs)
            q0s.append(pl.multiple_of(i * GRID_W, GRID_W))
            k0s.append(pl.multiple_of(rs * GRID_W, GRID_W))
        for u, (j, p) in enumerate(units):
            cs = slice(p * 2 * NA_HD, (p + 1) * 2 * NA_HD)
            qp = q_ref[0, pl.ds(q0s[j], GRID_W), cs].astype(F32)
            qq = jnp.concatenate([jnp.where(lo, qp, 0.0), jnp.where(lo, 0.0, qp)], axis=0).astype(BF16)
            s_scr[u] = _dot_nt(qq, k_ref[0, pl.ds(k0s[j], band), cs]) + bias_ref[p, ds[j]]
        for u in range(len(units)):
            s = s_scr[u]
            e = jnp.exp2(s - jnp.max(s, axis=-1, keepdims=True))
            l_scr[u] = jnp.broadcast_to(jnp.sum(e, axis=-1, keepdims=True), l_scr.shape[1:])
            e_scr[u] = e.astype(BF16)
        for u, (j, p) in enumerate(units):
            cs = slice(p * 2 * NA_HD, (p + 1) * 2 * NA_HD)
            o2 = _dot(e_scr[u], v_ref[0, pl.ds(k0s[j], band), cs]) / l_scr[u]
            op = jnp.where(lo, o2[0:GRID_W], o2[GRID_W:2 * GRID_W])
            o_ref[0, pl.ds(q0s[j], GRID_W), cs] = op.astype(BF16)
        return carry

    lax.fori_loop(0, NA_ROWS_PER_STEP // NA_ROWS_PER_ITER, rows_body, 0)


def _natten(qkv, th, *, layer):
    bsz, seq, d3 = qkv.shape
    d_n = d3 // 3
    rows = seq // GRID_W
    tq = NA_ROWS_PER_STEP * GRID_W
    n_units = NA_ROWS_PER_ITER * NA_HEADS // 2
    kern = functools.partial(_natten_kernel, rows=rows)
    return pl.pallas_call(
        kern,
        grid=(bsz, rows // NA_ROWS_PER_STEP),
        in_specs=[
            pl.BlockSpec((1, tq, d_n), lambda b, r: (b, r, 0)),
            pl.BlockSpec((1, seq, d_n), lambda b, r: (b, 0, 1)),
            pl.BlockSpec((1, seq, d_n), lambda b, r: (b, 0, 2)),
            pl.BlockSpec((None,) + th.shape[1:], lambda b, r: (layer, 0, 0, 0, 0)),
        ],
        out_specs=pl.BlockSpec((1, tq, d_n), lambda b, r: (b, r, 0)),
        out_shape=jax.ShapeDtypeStruct((bsz, seq, d_n), BF16),
        scratch_shapes=[
            pltpu.VMEM((NA_HEADS // 2, NA_KH, 2 * GRID_W, NA_KH * GRID_W), F32),
            pltpu.VMEM((n_units, 2 * GRID_W, NA_KH * GRID_W), F32),
            pltpu.VMEM((n_units, 2 * GRID_W, NA_KH * GRID_W), BF16),
            pltpu.VMEM((n_units, 2 * GRID_W, 2 * NA_HD), F32),
        ],
        compiler_params=pltpu.CompilerParams(
            dimension_semantics=("arbitrary", "arbitrary"), vmem_limit_bytes=VMEM_LIMIT),
        name="natten",
    )(qkv, qkv, qkv, th)


def _natten_bias_tables(rpb):
    depth, nh = rpb.shape[0], rpb.shape[1]
    n_rr, n_rc = 2 * NA_KH - 1, 2 * NA_KW - 1
    cols = np.arange(GRID_W)
    c_start = np.clip(cols - NA_KW // 2, 0, GRID_W - NA_KW)
    kc = np.arange(GRID_W)
    valid = (kc[None, :] >= c_start[:, None]) & (kc[None, :] < c_start[:, None] + NA_KW)
    rel_c = kc[None, :] - cols[:, None] + NA_KW - 1
    onehot = ((rel_c[None] == np.arange(n_rc)[:, None, None]) & valid[None]).astype(np.float32)
    t = jnp.dot(rpb.astype(F32).reshape(depth * nh * n_rr, n_rc), onehot.reshape(n_rc, GRID_W * GRID_W),
                precision=lax.Precision.HIGHEST).reshape(depth, nh, n_rr, GRID_W, GRID_W)
    t = t * LOG2E + np.where(valid, 0.0, NEG).astype(np.float32)
    t = t.transpose(0, 1, 3, 2, 4)
    t0 = t.reshape(depth, nh, GRID_W, n_rr * GRID_W)
    t1 = jnp.pad(t0[..., GRID_W:], ((0, 0), (0, 0), (0, 0), (0, GRID_W)))
    return jnp.stack([t0, t1], axis=1)


FF_CHUNK = 1024


def _out_ffn_kernel(x_ref, ym_ref, yn_ref, wo_ref, nw_ref, w1_ref, w2_ref, fw_ref, o_ref, x1_ref, *,
                    final):
    d_m = ym_ref.shape[1]
    x1_ref[...] = x_ref[...] + _dot(ym_ref[...], wo_ref[0:d_m, :]) + _dot(yn_ref[...], wo_ref[d_m:, :])
    h = _rms(x1_ref[...], nw_ref[...]).astype(BF16)
    d_ff = w1_ref.shape[1]
    ffn = None
    for j in range(d_ff // FF_CHUNK):
        cs = slice(j * FF_CHUNK, (j + 1) * FF_CHUNK)
        hid = jnp.square(jnp.maximum(_dot(h, w1_ref[:, cs]), 0.0)).astype(BF16)
        part = _dot(hid, w2_ref[cs, :])
        ffn = part if ffn is None else ffn + part
    acc = x1_ref[...] + ffn
    if final:
        acc = _rms(acc, fw_ref[...])
    o_ref[...] = acc


def _out_ffn(x2, ym, yn, wo, nw, w1, w2, fw, *, layer, final, tm):
    n, d = x2.shape
    d_m, d_n = ym.shape[1], yn.shape[1]
    kern = functools.partial(_out_ffn_kernel, final=final)
    layer_blk = lambda a: pl.BlockSpec((None,) + a.shape[1:], lambda i: (layer, 0, 0),
                                       pipeline_mode=pl.Buffered(1))
    return pl.pallas_call(
        kern,
        grid=(n // tm,),
        in_specs=[
            pl.BlockSpec((tm, d), lambda i: (i, 0)),
            pl.BlockSpec((tm, d_m), lambda i: (i, 0)),
            pl.BlockSpec((tm, d_n), lambda i: (i, 0)),
            layer_blk(wo), layer_blk(nw), layer_blk(w1), layer_blk(w2),
            pl.BlockSpec(memory_space=pltpu.VMEM),
        ],
        out_specs=pl.BlockSpec((tm, d), lambda i: (i, 0)),
        out_shape=jax.ShapeDtypeStruct((n, d), F32),
        scratch_shapes=[pltpu.VMEM((tm, d), F32)],
        compiler_params=pltpu.CompilerParams(
            dimension_semantics=("arbitrary",), vmem_limit_bytes=VMEM_LIMIT),
        name="out_ffn",
    )(x2, ym, yn, wo, nw, w1, w2, fw)


def kernel(x, norm1_w, w_in, conv_w, conv_b, gate_b, mlstm_norm_w, rpb, w_out, norm2_w, w_ff1, w_ff2,
           final_norm_w):
    bsz, seq, d = x.shape
    depth = w_in.shape[0]
    d_m = MLSTM_HEADS * MLSTM_HD
    d_n = NA_HEADS * NA_HD
    n = bsz * seq
    rows = seq // GRID_W
    nch = seq // LCH
    assert w_in.shape[2] == 4 * d_m + N_GATES + 3 * d_n
    assert seq % LCH == 0 and rows % NA_ROWS_PER_STEP == 0 and rows >= NA_KH

    g0 = 4 * d_m
    n0 = g0 + N_GATES
    w_m = w_in[:, :, 0:g0].astype(BF16)
    w_n = jnp.concatenate([w_in[:, :, n0:n0 + d_n] * (NA_HD ** -0.5 * LOG2E),
                           w_in[:, :, n0 + d_n:]], axis=2).astype(BF16)
    w_g = w_in[:, :, g0:n0].reshape(depth, d, 4, MLSTM_HEADS).transpose(0, 1, 3, 2)
    w_g = jnp.pad(w_g, ((0, 0), (0, 0), (0, 0), (0, GATES_PER_HEAD - 4))).reshape(depth, d, GATE_ROWS)
    w_g = jnp.pad(w_g, ((0, 0), (0, 0), (0, GATE_PAD - GATE_ROWS))).astype(BF16)
    w_o, w_1, w_2 = w_out.astype(BF16), w_ff1.astype(BF16), w_ff2.astype(BF16)
    n1, n2 = norm1_w[:, None, :], norm2_w[:, None, :]
    gb = gate_b.reshape(depth, 4, MLSTM_HEADS)
    cb, mn = conv_b[:, None, :], mlstm_norm_w[:, None, :]
    th = _natten_bias_tables(rpb)

    x2 = x.reshape(n, d)
    for l in range(depth):
        q, k, kt, va, o, qkv_n, gates = _in_proj(
            x2, n1, w_m, w_n, w_g, conv_w, cb, layer=l, seq=seq, d_m=d_m, tm=512)
        y_m = _mlstm(
            gb, q.reshape(bsz, seq, d_m), k.reshape(bsz, seq, d_m),
            kt.reshape(MLSTM_HEADS, bsz, nch, MLSTM_HD, LCH), va.reshape(bsz, seq, 2 * d_m),
            o.reshape(bsz, seq, d_m), gates.reshape(bsz, nch, GATE_ROWS, LCH), mn, layer=l)
        y_n = _natten(qkv_n.reshape(bsz, seq, 3 * d_n), th, layer=l)
        x2 = _out_ffn(
            x2, y_m.reshape(n, d_m), y_n.reshape(n, d_n), w_o, n2, w_1, w_2, final_norm_w[None],
            layer=l, final=(l == depth - 1), tm=512)
    return x2.reshape(bsz, seq, d)
```

```python
import functools

import numpy as np
import jax
import jax.numpy as jnp
from jax import lax
from jax.experimental import pallas as pl
from jax.experimental.pallas import tpu as pltpu

EPS = 1e-6
GRID_W = 64
MLSTM_HEADS = 4
MLSTM_HD = 128
NA_HEADS = 8
NA_HD = 64
NA_KH = 8
NA_KW = 16
CONV_K = 3
N_GATES = 4 * MLSTM_HEADS
LCH = 128
OUT_CHUNKS_PER_ITER = 4
NEG = -1e30
LOG2E = 1.4426950408889634

BF16 = jnp.bfloat16
F32 = jnp.float32

VMEM_LIMIT = 56 * 1024 * 1024


def _dot(a, b):
    return jnp.dot(a, b, preferred_element_type=F32)


def _dot_nt(a, b):
    return lax.dot_general(a, b, (((1,), (1,)), ((), ())), preferred_element_type=F32)


def _rms(x, w):
    return x * lax.rsqrt(jnp.mean(x * x, axis=-1, keepdims=True) + EPS) * w


_WA, _G, _LF, _TOT, _ML = range(5)
N_GATE_Q = 5


def _cumsum_lanes(x):
    lane = lax.broadcasted_iota(jnp.int32, x.shape, 1)
    k = 1
    while k < x.shape[1]:
        x = x + jnp.where(lane >= k, pltpu.roll(x, k, axis=1), 0.0)
        k *= 2
    return x


def _in_proj_kernel(x_ref, xp_ref, xn_ref, nw_ref, wm_ref, wn_ref, wg_ref, gb_ref, cw_ref, cb_ref,
                    q_ref, k_ref, kt_ref, va_ref, o_ref, n_ref, g_ref, *, d_m, tiles_per_seq):
    i = pl.program_id(0)
    tm = x_ref.shape[0]
    dh = MLSTM_HD
    nw = nw_ref[...]
    hn = jnp.concatenate(
        [_rms(x_ref[...], nw), _rms(xp_ref[...], nw), _rms(xn_ref[...], nw)], axis=0).astype(BF16)
    c0 = 2 * d_m
    first = jnp.where(i % tiles_per_seq == 0, 0.0, 1.0)
    last = jnp.where(i % tiles_per_seq == tiles_per_seq - 1, 0.0, 1.0)
    wb = 2 * dh
    row = lax.broadcasted_iota(jnp.int32, (tm, wb), 0)

    def conv_act(pre, c0_):
        cs = slice(c0_, c0_ + wb)
        x = pre[0:tm]
        xm1 = jnp.where(row == 0, pre[tm + HALO - 1:tm + HALO] * first, pltpu.roll(x, 1, axis=0))
        xp1 = jnp.where(row == tm - 1, pre[tm + HALO:tm + HALO + 1] * last, pltpu.roll(x, tm - 1, axis=0))
        y = cw_ref[0:1, cs] * xm1 + cw_ref[1:2, cs] * x + cw_ref[2:3, cs] * xp1 + cb_ref[:, cs]
        return y * jax.nn.sigmoid(y)

    hn_t = hn[0:tm]
    pre_q0 = _dot(hn, wm_ref[:, 0:wb])

    gt = _dot_nt(wg_ref[...], hn_t)
    is_bwd = lax.broadcasted_iota(jnp.int32, (2 * MLSTM_HEADS, LCH), 0) >= MLSTM_HEADS
    for j in range(g_ref.shape[0]):
        x = gt[:, j * LCH:(j + 1) * LCH] + gb_ref[...]
        ig = x[0:2 * MLSTM_HEADS] * LOG2E
        f = x[2 * MLSTM_HEADS:]
        lf = (jnp.minimum(f, 0.0) - jnp.log1p(jnp.exp(-jnp.abs(f)))) * LOG2E
        tot = jnp.sum(lf, axis=1, keepdims=True)
        csum = _cumsum_lanes(lf)
        b = jnp.where(is_bwd, tot - csum + lf, csum)
        a = tot - b + ig
        ml = jnp.max(a, axis=1, keepdims=True)
        g_ref[j, _WA] = jnp.exp2(a - ml)
        g_ref[j, _G] = ig - b
        g_ref[j, _LF] = lf
        g_ref[j, _TOT] = jnp.broadcast_to(tot, lf.shape)
        g_ref[j, _ML] = jnp.broadcast_to(ml, lf.shape)

    ones = jnp.ones((tm, dh), BF16)
    for hp in range(MLSTM_HEADS // 2):
        cs = slice(hp * wb, (hp + 1) * wb)
        q_ref[:, cs] = conv_act(pre_q0 if hp == 0 else _dot(hn, wm_ref[:, cs]), hp * wb)
        ka = conv_act(_dot(hn, wm_ref[:, d_m + hp * wb:d_m + (hp + 1) * wb]), d_m + hp * wb) * (dh ** -0.5)
        k_ref[:, cs] = ka.astype(BF16)
        v = _dot(hn_t, wm_ref[:, c0 + hp * wb:c0 + (hp + 1) * wb]).astype(BF16)
        for hh in range(2):
            h = 2 * hp + hh
            for j in range(tm // LCH):
                kt_ref[h, j] = ka[j * LCH:(j + 1) * LCH, hh * dh:(hh + 1) * dh].T
            va_ref[:, 2 * h * dh:(2 * h + 1) * dh] = v[:, hh * dh:(hh + 1) * dh]
            va_ref[:, (2 * h + 1) * dh:(2 * h + 2) * dh] = ones
    o_ref[...] = _dot(hn_t, wm_ref[:, c0 + d_m:c0 + 2 * d_m]).astype(BF16)
    n_ref[...] = _dot(hn_t, wn_ref[...]).astype(BF16)


HALO = 8


def _in_proj(x2, nw, wm, wn, wg, gb, cw, cb, *, layer, seq, d_m, tm):
    n, d = x2.shape
    d_n3 = wn.shape[2]
    nh, dh = MLSTM_HEADS, MLSTM_HD
    assert seq % tm == 0 and tm % LCH == 0
    kern = functools.partial(_in_proj_kernel, d_m=d_m, tiles_per_seq=seq // tm)
    layer_blk = lambda a: pl.BlockSpec((None,) + a.shape[1:], lambda i: (layer, 0, 0))
    hb = tm // HALO
    return pl.pallas_call(
        kern,
        grid=(n // tm,),
        in_specs=[
            pl.BlockSpec((tm, d), lambda i: (i, 0)),
            pl.BlockSpec((HALO, d), lambda i: (jnp.maximum(i * hb - 1, 0), 0)),
            pl.BlockSpec((HALO, d), lambda i: (jnp.minimum((i + 1) * hb, n // HALO - 1), 0)),
            layer_blk(nw), layer_blk(wm), layer_blk(wn), layer_blk(wg), layer_blk(gb),
            layer_blk(cw), layer_blk(cb),
        ],
        out_specs=[
            pl.BlockSpec((tm, d_m), lambda i: (i, 0)),
            pl.BlockSpec((tm, d_m), lambda i: (i, 0)),
            pl.BlockSpec((nh, tm // LCH, dh, LCH), lambda i: (0, i, 0, 0)),
            pl.BlockSpec((tm, 2 * d_m), lambda i: (i, 0)),
            pl.BlockSpec((tm, d_m), lambda i: (i, 0)),
            pl.BlockSpec((tm, d_n3), lambda i: (i, 0)),
            pl.BlockSpec((tm // LCH, N_GATE_Q, 2 * nh, LCH), lambda i: (i, 0, 0, 0)),
        ],
        out_shape=[
            jax.ShapeDtypeStruct((n, d_m), F32),
            jax.ShapeDtypeStruct((n, d_m), BF16),
            jax.ShapeDtypeStruct((nh, n // LCH, dh, LCH), F32),
            jax.ShapeDtypeStruct((n, 2 * d_m), BF16),
            jax.ShapeDtypeStruct((n, d_m), BF16),
            jax.ShapeDtypeStruct((n, d_n3), BF16),
            jax.ShapeDtypeStruct((n // LCH, N_GATE_Q, 2 * nh, LCH), F32),
        ],
        compiler_params=pltpu.CompilerParams(
            dimension_semantics=("arbitrary",), vmem_limit_bytes=VMEM_LIMIT),
        name="in_proj",
    )(x2, x2, x2, nw, wm, wn, wg, gb, cw, cb)


def _mlstm_kernel(q_ref, k_ref, kt_ref, va_ref, g_ref, h_ref, sf, sb, stf, stb, mp, *, nch):
    hd = pl.program_id(1)
    dh = MLSTM_HD
    gate_row = (hd, MLSTM_HEADS + hd)

    def grow(q_i, d, c):
        return g_ref[0, c, q_i, pl.ds(gate_row[d], 1), :]

    stf[...] = jnp.zeros_like(stf)
    stb[...] = jnp.zeros_like(stb)

    def scan_dir(c, m_prev, st, s_out, d):
        r0 = pl.multiple_of(c * LCH, LCH)
        tot = grow(_TOT, d, c)
        ml = grow(_ML, d, c)
        mp[d, pl.ds(c, 1), :] = m_prev
        s_prev = st[...]
        s_out[c] = s_prev.astype(BF16)
        a = (kt_ref[0, 0, c] * grow(_WA, d, c)).astype(BF16)
        u = _dot(a, va_ref[0, pl.ds(r0, LCH), :])
        m_new = jnp.maximum(tot + m_prev, ml)
        s_old = jnp.exp2(tot + m_prev - m_new)
        s_loc = jnp.exp2(ml - m_new)
        s_old2 = jnp.concatenate([s_old, s_old], axis=1)
        s_loc2 = jnp.concatenate([s_loc, s_loc], axis=1)
        st[...] = s_old2 * s_prev + s_loc2 * u
        return m_new

    def scan_body(i, carry):
        m_f, m_b = carry
        m_f = scan_dir(i, m_f, stf, sf, 0)
        m_b = scan_dir(nch - 1 - i, m_b, stb, sb, 1)
        return m_f, m_b

    zero_row = jnp.zeros((1, LCH), F32)
    lax.fori_loop(0, nch, scan_body, (zero_row, zero_row), unroll=4)

    jj = lax.broadcasted_iota(jnp.int32, (LCH, LCH), 0)
    ss = lax.broadcasted_iota(jnp.int32, (LCH, LCH), 1)
    mask_f = ss <= jj
    mask_b = ss >= jj

    dirs = ((mask_f, sf), (mask_b, sb))

    def weights_dir(c, q32, qk, mask, d):
        g_row = grow(_G, d, c)
        lf_row = grow(_LF, d, c)
        m_prev = mp[d, pl.ds(c, 1), :]
        gm = jnp.where(mask, g_row, -jnp.inf)
        mj = jnp.maximum(jnp.max(gm, axis=1, keepdims=True), m_prev)
        bj = jnp.sum(jnp.where(mask, lf_row, 0.0), axis=1, keepdims=True)
        p = (jnp.exp2(gm - mj) * qk).astype(BF16)
        wq = (jnp.exp2(m_prev - mj) * q32).astype(BF16)
        return jnp.concatenate([wq, p], axis=1), jnp.exp2(-bj - mj)

    def out_body(it, carry):
        cs = [it * OUT_CHUNKS_PER_ITER + j for j in range(OUT_CHUNKS_PER_ITER)]
        r0s = [pl.multiple_of(c * LCH, LCH) for c in cs]
        q32 = [q_ref[0, pl.ds(r0, LCH), :] for r0 in r0s]
        qk = [_dot_nt(q.astype(BF16), k_ref[0, pl.ds(r0, LCH), :]) for q, r0 in zip(q32, r0s)]
        wts = [[weights_dir(c, q, s, mask, d) for d, (mask, _) in enumerate(dirs)]
               for c, q, s in zip(cs, q32, qk)]
        for c, r0, wt in zip(cs, r0s, wts):
            vc = va_ref[0, pl.ds(r0, LCH), :]
            h = None
            for (lhs, bound), (_, s_ref) in zip(wt, dirs):
                tot = _dot(lhs, jnp.concatenate([s_ref[c], vc], axis=0))
                hd_ = tot[:, 0:dh] / jnp.maximum(jnp.abs(tot[:, dh:2 * dh]), bound)
                h = hd_ if h is None else h + hd_
            h_ref[0, pl.ds(r0, LCH), :] = h.astype(BF16)
        return carry

    lax.fori_loop(0, nch // OUT_CHUNKS_PER_ITER, out_body, 0)


def _mlstm(q, k, kt, va, gates):
    bsz, seq, d_m = q.shape
    nh, dh = MLSTM_HEADS, MLSTM_HD
    nch = seq // LCH
    kern = functools.partial(_mlstm_kernel, nch=nch)
    seq_blk = lambda width: pl.BlockSpec((1, seq, width), lambda b, h: (b, 0, h))
    return pl.pallas_call(
        kern,
        grid=(bsz, nh),
        in_specs=[
            seq_blk(dh), seq_blk(dh),
            pl.BlockSpec((1, 1, nch, dh, LCH), lambda b, h: (h, b, 0, 0, 0)),
            seq_blk(2 * dh),
            pl.BlockSpec((1, nch, N_GATE_Q, 2 * nh, LCH), lambda b, h: (b, 0, 0, 0, 0)),
        ],
        out_specs=pl.BlockSpec((1, seq, dh), lambda b, h: (b, 0, h)),
        out_shape=jax.ShapeDtypeStruct((bsz, seq, d_m), BF16),
        scratch_shapes=[
            pltpu.VMEM((nch, dh, 2 * dh), BF16),
            pltpu.VMEM((nch, dh, 2 * dh), BF16),
            pltpu.VMEM((dh, 2 * dh), F32),
            pltpu.VMEM((dh, 2 * dh), F32),
            pltpu.VMEM((2, nch, LCH), F32),
        ],
        compiler_params=pltpu.CompilerParams(
            dimension_semantics=("arbitrary", "arbitrary"), vmem_limit_bytes=VMEM_LIMIT),
        name="mlstm",
    )(q, k, kt, va, gates)


NA_ROWS_PER_STEP = 8
NA_ROWS_PER_ITER = 2


def _natten_kernel(q_ref, k_ref, v_ref, th_ref, o_ref, bias_ref, s_scr, e_scr, l_scr, *, rows):
    rb = pl.program_id(1)
    band = NA_KH * GRID_W
    npair = NA_HEADS // 2
    lane = lax.broadcasted_iota(jnp.int32, (GRID_W, 2 * NA_HD), 1)
    lo = lane < NA_HD
    units = [(j, p) for j in range(NA_ROWS_PER_ITER) for p in range(npair)]

    @pl.when((pl.program_id(0) == 0) & (rb == 0))
    def _():
        for h in range(NA_HEADS):
            for d in range(NA_KH):
                first = NA_KH - 1 - d
                par = first % 2
                off = (first - par) * GRID_W
                bias_ref[h // 2, d, (h % 2) * GRID_W:(h % 2 + 1) * GRID_W, :] = (
                    th_ref[par, h, :, off:off + band])

    def rows_body(it, carry):
        q0s, k0s, ds = [], [], []
        for j in range(NA_ROWS_PER_ITER):
            i = it * NA_ROWS_PER_ITER + j
            r = rb * NA_ROWS_PER_STEP + i
            rs = jnp.clip(r - NA_KH // 2, 0, rows - NA_KH)
            ds.append(r - rs)
            q0s.append(pl.multiple_of(i * GRID_W, GRID_W))
            k0s.append(pl.multiple_of(rs * GRID_W, GRID_W))
        for u, (j, p) in enumerate(units):
            cs = slice(p * 2 * NA_HD, (p + 1) * 2 * NA_HD)
            qp = q_ref[0, pl.ds(q0s[j], GRID_W), cs].astype(F32)
            qq = jnp.concatenate([jnp.where(lo, qp, 0.0), jnp.where(lo, 0.0, qp)], axis=0).astype(BF16)
            s_scr[u] = _dot_nt(qq, k_ref[0, pl.ds(k0s[j], band), cs]) + bias_ref[p, ds[j]]
        for u in range(len(units)):
            s = s_scr[u]
            e = jnp.exp2(s - jnp.max(s, axis=-1, keepdims=True))
            l_scr[u] = jnp.broadcast_to(jnp.sum(e, axis=-1, keepdims=True), l_scr.shape[1:])
            e_scr[u] = e.astype(BF16)
        for u, (j, p) in enumerate(units):
            cs = slice(p * 2 * NA_HD, (p + 1) * 2 * NA_HD)
            o2 = _dot(e_scr[u], v_ref[0, pl.ds(k0s[j], band), cs]) / l_scr[u]
            op = jnp.where(lo, o2[0:GRID_W], o2[GRID_W:2 * GRID_W])
            o_ref[0, pl.ds(q0s[j], GRID_W), cs] = op.astype(BF16)
        return carry

    lax.fori_loop(0, NA_ROWS_PER_STEP // NA_ROWS_PER_ITER, rows_body, 0)


def _natten(qkv, th, *, layer):
    bsz, seq, d3 = qkv.shape
    d_n = d3 // 3
    rows = seq // GRID_W
    tq = NA_ROWS_PER_STEP * GRID_W
    n_units = NA_ROWS_PER_ITER * NA_HEADS // 2
    kern = functools.partial(_natten_kernel, rows=rows)
    return pl.pallas_call(
        kern,
        grid=(bsz, rows // NA_ROWS_PER_STEP),
        in_specs=[
            pl.BlockSpec((1, tq, d_n), lambda b, r: (b, r, 0)),
            pl.BlockSpec((1, seq, d_n), lambda b, r: (b, 0, 1)),
            pl.BlockSpec((1, seq, d_n), lambda b, r: (b, 0, 2)),
            pl.BlockSpec((None,) + th.shape[1:], lambda b, r: (layer, 0, 0, 0, 0)),
        ],
        out_specs=pl.BlockSpec((1, tq, d_n), lambda b, r: (b, r, 0)),
        out_shape=jax.ShapeDtypeStruct((bsz, seq, d_n), BF16),
        scratch_shapes=[
            pltpu.VMEM((NA_HEADS // 2, NA_KH, 2 * GRID_W, NA_KH * GRID_W), F32),
            pltpu.VMEM((n_units, 2 * GRID_W, NA_KH * GRID_W), F32),
            pltpu.VMEM((n_units, 2 * GRID_W, NA_KH * GRID_W), BF16),
            pltpu.VMEM((n_units, 2 * GRID_W, 2 * NA_HD), F32),
        ],
        compiler_params=pltpu.CompilerParams(
            dimension_semantics=("arbitrary", "arbitrary"), vmem_limit_bytes=VMEM_LIMIT),
        name="natten",
    )(qkv, qkv, qkv, th)


def _natten_bias_tables(rpb):
    depth, nh = rpb.shape[0], rpb.shape[1]
    n_rr, n_rc = 2 * NA_KH - 1, 2 * NA_KW - 1
    cols = np.arange(GRID_W)
    c_start = np.clip(cols - NA_KW // 2, 0, GRID_W - NA_KW)
    kc = np.arange(GRID_W)
    valid = (kc[None, :] >= c_start[:, None]) & (kc[None, :] < c_start[:, None] + NA_KW)
    rel_c = kc[None, :] - cols[:, None] + NA_KW - 1
    onehot = ((rel_c[None] == np.arange(n_rc)[:, None, None]) & valid[None]).astype(np.float32)
    t = jnp.dot(rpb.astype(F32).reshape(depth * nh * n_rr, n_rc), onehot.reshape(n_rc, GRID_W * GRID_W),
                precision=lax.Precision.HIGHEST).reshape(depth, nh, n_rr, GRID_W, GRID_W)
    t = t * LOG2E + np.where(valid, 0.0, NEG).astype(np.float32)
    t = t.transpose(0, 1, 3, 2, 4)
    t0 = t.reshape(depth, nh, GRID_W, n_rr * GRID_W)
    t1 = jnp.pad(t0[..., GRID_W:], ((0, 0), (0, 0), (0, 0), (0, GRID_W)))
    return jnp.stack([t0, t1], axis=1)


FF_CHUNK = 1024


def _out_ffn_kernel(x_ref, hm_ref, om_ref, yn_ref, mw_ref, wo_ref, nw_ref, w1_ref, w2_ref, fw_ref,
                    o_ref, x1_ref, *, final):
    d_m = hm_ref.shape[1]
    dh = MLSTM_HD
    hm = hm_ref[...].astype(F32)
    hn = jnp.concatenate(
        [hm[:, h * dh:(h + 1) * dh]
         * lax.rsqrt(jnp.mean(jnp.square(hm[:, h * dh:(h + 1) * dh]), axis=-1, keepdims=True) + EPS)
         for h in range(d_m // dh)], axis=1) * mw_ref[...]
    ym = (jax.nn.sigmoid(om_ref[...].astype(F32)) * hn).astype(BF16)
    x1_ref[...] = x_ref[...] + _dot(yn_ref[...], wo_ref[d_m:, :]) + _dot(ym, wo_ref[0:d_m, :])
    h = _rms(x1_ref[...], nw_ref[...]).astype(BF16)
    d_ff = w1_ref.shape[1]
    ffn = None
    for j in range(d_ff // FF_CHUNK):
        cs = slice(j * FF_CHUNK, (j + 1) * FF_CHUNK)
        hid = jnp.square(jnp.maximum(_dot(h, w1_ref[:, cs]), 0.0)).astype(BF16)
        part = _dot(hid, w2_ref[cs, :])
        ffn = part if ffn is None else ffn + part
    acc = x1_ref[...] + ffn
    if final:
        acc = _rms(acc, fw_ref[...])
    o_ref[...] = acc


def _out_ffn(x2, hm, om, yn, mw, wo, nw, w1, w2, fw, *, layer, final, tm):
    n, d = x2.shape
    d_m, d_n = hm.shape[1], yn.shape[1]
    kern = functools.partial(_out_ffn_kernel, final=final)
    layer_blk = lambda a: pl.BlockSpec((None,) + a.shape[1:], lambda i: (layer, 0, 0),
                                       pipeline_mode=pl.Buffered(1))
    return pl.pallas_call(
        kern,
        grid=(n // tm,),
        in_specs=[
            pl.BlockSpec((tm, d), lambda i: (i, 0)),
            pl.BlockSpec((tm, d_m), lambda i: (i, 0)),
            pl.BlockSpec((tm, d_m), lambda i: (i, 0)),
            pl.BlockSpec((tm, d_n), lambda i: (i, 0)),
            layer_blk(mw), layer_blk(wo), layer_blk(nw), layer_blk(w1), layer_blk(w2),
            pl.BlockSpec(memory_space=pltpu.VMEM),
        ],
        out_specs=pl.BlockSpec((tm, d), lambda i: (i, 0)),
        out_shape=jax.ShapeDtypeStruct((n, d), F32),
        scratch_shapes=[pltpu.VMEM((tm, d), F32)],
        compiler_params=pltpu.CompilerParams(
            dimension_semantics=("arbitrary",), vmem_limit_bytes=VMEM_LIMIT),
        name="out_ffn",
    )(x2, hm, om, yn, mw, wo, nw, w1, w2, fw)


def kernel(x, norm1_w, w_in, conv_w, conv_b, gate_b, mlstm_norm_w, rpb, w_out, norm2_w, w_ff1, w_ff2,
           final_norm_w):
    bsz, seq, d = x.shape
    depth = w_in.shape[0]
    d_m = MLSTM_HEADS * MLSTM_HD
    d_n = NA_HEADS * NA_HD
    n = bsz * seq
    rows = seq // GRID_W
    nch = seq // LCH
    assert w_in.shape[2] == 4 * d_m + N_GATES + 3 * d_n
    assert seq % LCH == 0 and rows % NA_ROWS_PER_STEP == 0 and rows >= NA_KH

    g0 = 4 * d_m
    n0 = g0 + N_GATES
    w_m = w_in[:, :, 0:g0].astype(BF16)
    w_n = jnp.concatenate([w_in[:, :, n0:n0 + d_n] * (NA_HD ** -0.5 * LOG2E),
                           w_in[:, :, n0 + d_n:]], axis=2).astype(BF16)
    gate_order = np.array([0, 2, 1, 3])
    w_g = w_in[:, :, g0:n0].reshape(depth, d, 4, MLSTM_HEADS)[:, :, gate_order, :]
    w_g = w_g.reshape(depth, d, N_GATES).transpose(0, 2, 1).astype(BF16)
    gb = gate_b.reshape(depth, 4, MLSTM_HEADS)[:, gate_order, :].reshape(depth, N_GATES, 1)
    gb = jnp.broadcast_to(gb, (depth, N_GATES, LCH))
    w_o, w_1, w_2 = w_out.astype(BF16), w_ff1.astype(BF16), w_ff2.astype(BF16)
    n1, n2 = norm1_w[:, None, :], norm2_w[:, None, :]
    cb, mn = conv_b[:, None, :], mlstm_norm_w[:, None, :]
    th = _natten_bias_tables(rpb)

    x2 = x.reshape(n, d)
    for l in range(depth):
        q, k, kt, va, o, qkv_n, gates = _in_proj(
            x2, n1, w_m, w_n, w_g, gb, conv_w, cb, layer=l, seq=seq, d_m=d_m, tm=512)
        h_m = _mlstm(
            q.reshape(bsz, seq, d_m), k.reshape(bsz, seq, d_m),
            kt.reshape(MLSTM_HEADS, bsz, nch, MLSTM_HD, LCH), va.reshape(bsz, seq, 2 * d_m),
            gates.reshape(bsz, nch, N_GATE_Q, 2 * MLSTM_HEADS, LCH))
        y_n = _natten(qkv_n.reshape(bsz, seq, 3 * d_n), th, layer=l)
        x2 = _out_ffn(
            x2, h_m.reshape(n, d_m), o, y_n.reshape(n, d_n), mn, w_o, n2, w_1, w_2, final_norm_w[None],
            layer=l, final=(l == depth - 1), tm=512)
    return x2.reshape(bsz, seq, d)
```

```python
import functools

import numpy as np
import jax
import jax.numpy as jnp
from jax import lax
from jax.experimental import pallas as pl
from jax.experimental.pallas import tpu as pltpu

EPS = 1e-6
GRID_W = 64
MLSTM_HEADS = 4
MLSTM_HD = 128
NA_HEADS = 8
NA_HD = 64
NA_KH = 8
NA_KW = 16
CONV_K = 3
N_GATES = 4 * MLSTM_HEADS
LCH = 128
OUT_CHUNKS_PER_ITER = 4
NEG = -1e30
LOG2E = 1.4426950408889634

BF16 = jnp.bfloat16
F32 = jnp.float32

VMEM_LIMIT = 56 * 1024 * 1024


def _dot(a, b):
    return jnp.dot(a, b, preferred_element_type=F32)


def _dot_nt(a, b):
    return lax.dot_general(a, b, (((1,), (1,)), ((), ())), preferred_element_type=F32)


def _rms(x, w):
    return x * lax.rsqrt(jnp.mean(x * x, axis=-1, keepdims=True) + EPS) * w


_WA, _G, _LF, _TOT, _ML = range(5)
N_GATE_Q = 5


def _cumsum_lanes(x):
    lane = lax.broadcasted_iota(jnp.int32, x.shape, 1)
    k = 1
    while k < x.shape[1]:
        x = x + jnp.where(lane >= k, pltpu.roll(x, k, axis=1), 0.0)
        k *= 2
    return x


def _in_proj_kernel(x_ref, xp_ref, xn_ref, nw_ref, wm_ref, wn_ref, wg_ref, gb_ref, cw_ref, cb_ref,
                    q_ref, k_ref, kt_ref, va_ref, o_ref, n_ref, g_ref, *, d_m, tm, steps_per_seq):
    i = pl.program_id(0)
    nsub = x_ref.shape[0] // tm
    dh = MLSTM_HD
    nw = nw_ref[...]
    c0 = 2 * d_m
    wb = 2 * dh
    row = lax.broadcasted_iota(jnp.int32, (tm, wb), 0)
    is_bwd = lax.broadcasted_iota(jnp.int32, (2 * MLSTM_HEADS, LCH), 0) >= MLSTM_HEADS
    ones = jnp.ones((tm, dh), BF16)
    seq_start = jnp.where(i % steps_per_seq == 0, 0.0, 1.0)
    seq_end = jnp.where(i % steps_per_seq == steps_per_seq - 1, 0.0, 1.0)

    for t in range(nsub):
        r0 = t * tm
        rs = slice(r0, r0 + tm)
        j0 = r0 // LCH
        x_prev = xp_ref[...] if t == 0 else x_ref[r0 - HALO:r0, :]
        x_next = xn_ref[...] if t == nsub - 1 else x_ref[r0 + tm:r0 + tm + HALO, :]
        first = seq_start if t == 0 else 1.0
        last = seq_end if t == nsub - 1 else 1.0
        hn = jnp.concatenate(
            [_rms(x_ref[rs, :], nw), _rms(x_prev, nw), _rms(x_next, nw)], axis=0).astype(BF16)

        def conv_act(pre, c0_):
            cs = slice(c0_, c0_ + wb)
            x = pre[0:tm]
            xm1 = jnp.where(row == 0, pre[tm + HALO - 1:tm + HALO] * first, pltpu.roll(x, 1, axis=0))
            xp1 = jnp.where(row == tm - 1, pre[tm + HALO:tm + HALO + 1] * last, pltpu.roll(x, tm - 1, axis=0))
            y = cw_ref[0:1, cs] * xm1 + cw_ref[1:2, cs] * x + cw_ref[2:3, cs] * xp1 + cb_ref[:, cs]
            return y * jax.nn.sigmoid(y)

        hn_t = hn[0:tm]
        pre_q0 = _dot(hn, wm_ref[:, 0:wb])

        gt = _dot_nt(wg_ref[...], hn_t)
        for j in range(tm // LCH):
            x = gt[:, j * LCH:(j + 1) * LCH] + gb_ref[...]
            ig = x[0:2 * MLSTM_HEADS] * LOG2E
            f = x[2 * MLSTM_HEADS:]
            lf = (jnp.minimum(f, 0.0) - jnp.log1p(jnp.exp(-jnp.abs(f)))) * LOG2E
            tot = jnp.sum(lf, axis=1, keepdims=True)
            csum = _cumsum_lanes(lf)
            b = jnp.where(is_bwd, tot - csum + lf, csum)
            a = tot - b + ig
            ml = jnp.max(a, axis=1, keepdims=True)
            g_ref[j0 + j, _WA] = jnp.exp2(a - ml)
            g_ref[j0 + j, _G] = ig - b
            g_ref[j0 + j, _LF] = lf
            g_ref[j0 + j, _TOT] = jnp.broadcast_to(tot, lf.shape)
            g_ref[j0 + j, _ML] = jnp.broadcast_to(ml, lf.shape)

        for hp in range(MLSTM_HEADS // 2):
            cs = slice(hp * wb, (hp + 1) * wb)
            q_ref[rs, cs] = conv_act(pre_q0 if hp == 0 else _dot(hn, wm_ref[:, cs]), hp * wb)
            ka = conv_act(_dot(hn, wm_ref[:, d_m + hp * wb:d_m + (hp + 1) * wb]), d_m + hp * wb) * (dh ** -0.5)
            k_ref[rs, cs] = ka.astype(BF16)
            v = _dot(hn_t, wm_ref[:, c0 + hp * wb:c0 + (hp + 1) * wb]).astype(BF16)
            for hh in range(2):
                h = 2 * hp + hh
                for j in range(tm // LCH):
                    kt_ref[h, j0 + j] = ka[j * LCH:(j + 1) * LCH, hh * dh:(hh + 1) * dh].T
                va_ref[rs, 2 * h * dh:(2 * h + 1) * dh] = v[:, hh * dh:(hh + 1) * dh]
                va_ref[rs, (2 * h + 1) * dh:(2 * h + 2) * dh] = ones
        o_ref[rs, :] = _dot(hn_t, wm_ref[:, c0 + d_m:c0 + 2 * d_m]).astype(BF16)
        n_ref[rs, :] = _dot(hn_t, wn_ref[...]).astype(BF16)


HALO = 8
SUBTILES = 2


def _in_proj(x2, nw, wm, wn, wg, gb, cw, cb, *, layer, seq, d_m, tm):
    n, d = x2.shape
    d_n3 = wn.shape[2]
    nh, dh = MLSTM_HEADS, MLSTM_HD
    tb = SUBTILES * tm
    assert seq % tb == 0 and tm % LCH == 0
    kern = functools.partial(_in_proj_kernel, d_m=d_m, tm=tm, steps_per_seq=seq // tb)
    layer_blk = lambda a: pl.BlockSpec((None,) + a.shape[1:], lambda i: (layer, 0, 0),
                                       pipeline_mode=pl.Buffered(1))
    hb = tb // HALO
    return pl.pallas_call(
        kern,
        grid=(n // tb,),
        in_specs=[
            pl.BlockSpec((tb, d), lambda i: (i, 0)),
            pl.BlockSpec((HALO, d), lambda i: (jnp.maximum(i * hb - 1, 0), 0)),
            pl.BlockSpec((HALO, d), lambda i: (jnp.minimum((i + 1) * hb, n // HALO - 1), 0)),
            layer_blk(nw), layer_blk(wm), layer_blk(wn), layer_blk(wg), layer_blk(gb),
            layer_blk(cw), layer_blk(cb),
        ],
        out_specs=[
            pl.BlockSpec((tb, d_m), lambda i: (i, 0)),
            pl.BlockSpec((tb, d_m), lambda i: (i, 0)),
            pl.BlockSpec((nh, tb // LCH, dh, LCH), lambda i: (0, i, 0, 0)),
            pl.BlockSpec((tb, 2 * d_m), lambda i: (i, 0)),
            pl.BlockSpec((tb, d_m), lambda i: (i, 0)),
            pl.BlockSpec((tb, d_n3), lambda i: (i, 0)),
            pl.BlockSpec((tb // LCH, N_GATE_Q, 2 * nh, LCH), lambda i: (i, 0, 0, 0)),
        ],
        out_shape=[
            jax.ShapeDtypeStruct((n, d_m), F32),
            jax.ShapeDtypeStruct((n, d_m), BF16),
            jax.ShapeDtypeStruct((nh, n // LCH, dh, LCH), F32),
            jax.ShapeDtypeStruct((n, 2 * d_m), BF16),
            jax.ShapeDtypeStruct((n, d_m), BF16),
            jax.ShapeDtypeStruct((n, d_n3), BF16),
            jax.ShapeDtypeStruct((n // LCH, N_GATE_Q, 2 * nh, LCH), F32),
        ],
        compiler_params=pltpu.CompilerParams(
            dimension_semantics=("arbitrary",), vmem_limit_bytes=VMEM_LIMIT),
        name="in_proj",
    )(x2, x2, x2, nw, wm, wn, wg, gb, cw, cb)


def _mlstm_kernel(q_ref, k_ref, kt_ref, va_ref, g_ref, h_ref, sf, sb, stf, stb, mp, *, nch):
    hd = pl.program_id(1)
    dh = MLSTM_HD
    gate_row = (hd, MLSTM_HEADS + hd)

    def grow(q_i, d, c):
        return g_ref[0, c, q_i, pl.ds(gate_row[d], 1), :]

    stf[...] = jnp.zeros_like(stf)
    stb[...] = jnp.zeros_like(stb)

    def scan_dir(c, m_prev, st, s_out, d):
        r0 = pl.multiple_of(c * LCH, LCH)
        tot = grow(_TOT, d, c)
        ml = grow(_ML, d, c)
        mp[d, pl.ds(c, 1), :] = m_prev
        s_prev = st[...]
        s_out[c] = s_prev.astype(BF16)
        a = (kt_ref[0, 0, c] * grow(_WA, d, c)).astype(BF16)
        u = _dot(a, va_ref[0, pl.ds(r0, LCH), :])
        m_new = jnp.maximum(tot + m_prev, ml)
        s_old = jnp.exp2(tot + m_prev - m_new)
        s_loc = jnp.exp2(ml - m_new)
        s_old2 = jnp.concatenate([s_old, s_old], axis=1)
        s_loc2 = jnp.concatenate([s_loc, s_loc], axis=1)
        st[...] = s_old2 * s_prev + s_loc2 * u
        return m_new

    def scan_body(i, carry):
        m_f, m_b = carry
        m_f = scan_dir(i, m_f, stf, sf, 0)
        m_b = scan_dir(nch - 1 - i, m_b, stb, sb, 1)
        return m_f, m_b

    zero_row = jnp.zeros((1, LCH), F32)
    lax.fori_loop(0, nch, scan_body, (zero_row, zero_row), unroll=4)

    jj = lax.broadcasted_iota(jnp.int32, (LCH, LCH), 0)
    ss = lax.broadcasted_iota(jnp.int32, (LCH, LCH), 1)
    mask_f = ss <= jj
    mask_b = ss >= jj

    dirs = ((mask_f, sf), (mask_b, sb))

    def weights_dir(c, q32, qk, mask, d):
        g_row = grow(_G, d, c)
        lf_row = grow(_LF, d, c)
        m_prev = mp[d, pl.ds(c, 1), :]
        gm = jnp.where(mask, g_row, -jnp.inf)
        mj = jnp.maximum(jnp.max(gm, axis=1, keepdims=True), m_prev)
        bj = jnp.sum(jnp.where(mask, lf_row, 0.0), axis=1, keepdims=True)
        p = (jnp.exp2(gm - mj) * qk).astype(BF16)
        wq = (jnp.exp2(m_prev - mj) * q32).astype(BF16)
        return jnp.concatenate([wq, p], axis=1), jnp.exp2(-bj - mj)

    def out_body(it, carry):
        cs = [it * OUT_CHUNKS_PER_ITER + j for j in range(OUT_CHUNKS_PER_ITER)]
        r0s = [pl.multiple_of(c * LCH, LCH) for c in cs]
        q32 = [q_ref[0, pl.ds(r0, LCH), :] for r0 in r0s]
        qk = [_dot_nt(q.astype(BF16), k_ref[0, pl.ds(r0, LCH), :]) for q, r0 in zip(q32, r0s)]
        wts = [[weights_dir(c, q, s, mask, d) for d, (mask, _) in enumerate(dirs)]
               for c, q, s in zip(cs, q32, qk)]
        for c, r0, wt in zip(cs, r0s, wts):
            vc = va_ref[0, pl.ds(r0, LCH), :]
            h = None
            for (lhs, bound), (_, s_ref) in zip(wt, dirs):
                tot = _dot(lhs, jnp.concatenate([s_ref[c], vc], axis=0))
                hd_ = tot[:, 0:dh] / jnp.maximum(jnp.abs(tot[:, dh:2 * dh]), bound)
                h = hd_ if h is None else h + hd_
            h_ref[0, pl.ds(r0, LCH), :] = h.astype(BF16)
        return carry

    lax.fori_loop(0, nch // OUT_CHUNKS_PER_ITER, out_body, 0)


def _mlstm(q, k, kt, va, gates):
    bsz, seq, d_m = q.shape
    nh, dh = MLSTM_HEADS, MLSTM_HD
    nch = seq // LCH
    kern = functools.partial(_mlstm_kernel, nch=nch)
    seq_blk = lambda width: pl.BlockSpec((1, seq, width), lambda b, h: (b, 0, h))
    return pl.pallas_call(
        kern,
        grid=(bsz, nh),
        in_specs=[
            seq_blk(dh), seq_blk(dh),
            pl.BlockSpec((1, 1, nch, dh, LCH), lambda b, h: (h, b, 0, 0, 0)),
            seq_blk(2 * dh),
            pl.BlockSpec((1, nch, N_GATE_Q, 2 * nh, LCH), lambda b, h: (b, 0, 0, 0, 0)),
        ],
        out_specs=pl.BlockSpec((1, seq, dh), lambda b, h: (b, 0, h)),
        out_shape=jax.ShapeDtypeStruct((bsz, seq, d_m), BF16),
        scratch_shapes=[
            pltpu.VMEM((nch, dh, 2 * dh), BF16),
            pltpu.VMEM((nch, dh, 2 * dh), BF16),
            pltpu.VMEM((dh, 2 * dh), F32),
            pltpu.VMEM((dh, 2 * dh), F32),
            pltpu.VMEM((2, nch, LCH), F32),
        ],
        compiler_params=pltpu.CompilerParams(
            dimension_semantics=("arbitrary", "arbitrary"), vmem_limit_bytes=VMEM_LIMIT),
        name="mlstm",
    )(q, k, kt, va, gates)


NA_ROWS_PER_STEP = 8
NA_ROWS_PER_ITER = 2


def _natten_kernel(q_ref, k_ref, v_ref, th_ref, o_ref, bias_ref, s_scr, e_scr, l_scr, *, rows):
    rb = pl.program_id(1)
    band = NA_KH * GRID_W
    npair = NA_HEADS // 2
    lane = lax.broadcasted_iota(jnp.int32, (GRID_W, 2 * NA_HD), 1)
    lo = lane < NA_HD
    units = [(j, p) for j in range(NA_ROWS_PER_ITER) for p in range(npair)]

    @pl.when((pl.program_id(0) == 0) & (rb == 0))
    def _():
        for h in range(NA_HEADS):
            for d in range(NA_KH):
                first = NA_KH - 1 - d
                par = first % 2
                off = (first - par) * GRID_W
                bias_ref[h // 2, d, (h % 2) * GRID_W:(h % 2 + 1) * GRID_W, :] = (
                    th_ref[par, h, :, off:off + band])

    def rows_body(it, carry):
        q0s, k0s, ds = [], [], []
        for j in range(NA_ROWS_PER_ITER):
            i = it * NA_ROWS_PER_ITER + j
            r = rb * NA_ROWS_PER_STEP + i
            rs = jnp.clip(r - NA_KH // 2, 0, rows - NA_KH)
            ds.append(r - rs)
            q0s.append(pl.multiple_of(i * GRID_W, GRID_W))
            k0s.append(pl.multiple_of(rs * GRID_W, GRID_W))
        for u, (j, p) in enumerate(units):
            cs = slice(p * 2 * NA_HD, (p + 1) * 2 * NA_HD)
            qp = q_ref[0, pl.ds(q0s[j], GRID_W), cs].astype(F32)
            qq = jnp.concatenate([jnp.where(lo, qp, 0.0), jnp.where(lo, 0.0, qp)], axis=0).astype(BF16)
            s_scr[u] = _dot_nt(qq, k_ref[0, pl.ds(k0s[j], band), cs]) + bias_ref[p, ds[j]]
        for u in range(len(units)):
            s = s_scr[u]
            e = jnp.exp2(s - jnp.max(s, axis=-1, keepdims=True))
            l_scr[u] = jnp.broadcast_to(jnp.sum(e, axis=-1, keepdims=True), l_scr.shape[1:])
            e_scr[u] = e.astype(BF16)
        for u, (j, p) in enumerate(units):
            cs = slice(p * 2 * NA_HD, (p + 1) * 2 * NA_HD)
            o2 = _dot(e_scr[u], v_ref[0, pl.ds(k0s[j], band), cs]) / l_scr[u]
            op = jnp.where(lo, o2[0:GRID_W], o2[GRID_W:2 * GRID_W])
            o_ref[0, pl.ds(q0s[j], GRID_W), cs] = op.astype(BF16)
        return carry

    lax.fori_loop(0, NA_ROWS_PER_STEP // NA_ROWS_PER_ITER, rows_body, 0)


def _natten(qkv, th, *, layer):
    bsz, seq, d3 = qkv.shape
    d_n = d3 // 3
    rows = seq // GRID_W
    tq = NA_ROWS_PER_STEP * GRID_W
    n_units = NA_ROWS_PER_ITER * NA_HEADS // 2
    kern = functools.partial(_natten_kernel, rows=rows)
    return pl.pallas_call(
        kern,
        grid=(bsz, rows // NA_ROWS_PER_STEP),
        in_specs=[
            pl.BlockSpec((1, tq, d_n), lambda b, r: (b, r, 0)),
            pl.BlockSpec((1, seq, d_n), lambda b, r: (b, 0, 1)),
            pl.BlockSpec((1, seq, d_n), lambda b, r: (b, 0, 2)),
            pl.BlockSpec((None,) + th.shape[1:], lambda b, r: (layer, 0, 0, 0, 0)),
        ],
        out_specs=pl.BlockSpec((1, tq, d_n), lambda b, r: (b, r, 0)),
        out_shape=jax.ShapeDtypeStruct((bsz, seq, d_n), BF16),
        scratch_shapes=[
            pltpu.VMEM((NA_HEADS // 2, NA_KH, 2 * GRID_W, NA_KH * GRID_W), F32),
            pltpu.VMEM((n_units, 2 * GRID_W, NA_KH * GRID_W), F32),
            pltpu.VMEM((n_units, 2 * GRID_W, NA_KH * GRID_W), BF16),
            pltpu.VMEM((n_units, 2 * GRID_W, 2 * NA_HD), F32),
        ],
        compiler_params=pltpu.CompilerParams(
            dimension_semantics=("arbitrary", "arbitrary"), vmem_limit_bytes=VMEM_LIMIT),
        name="natten",
    )(qkv, qkv, qkv, th)


def _natten_bias_tables(rpb):
    depth, nh = rpb.shape[0], rpb.shape[1]
    n_rr, n_rc = 2 * NA_KH - 1, 2 * NA_KW - 1
    cols = np.arange(GRID_W)
    c_start = np.clip(cols - NA_KW // 2, 0, GRID_W - NA_KW)
    kc = np.arange(GRID_W)
    valid = (kc[None, :] >= c_start[:, None]) & (kc[None, :] < c_start[:, None] + NA_KW)
    rel_c = kc[None, :] - cols[:, None] + NA_KW - 1
    onehot = ((rel_c[None] == np.arange(n_rc)[:, None, None]) & valid[None]).astype(np.float32)
    t = jnp.dot(rpb.astype(F32).reshape(depth * nh * n_rr, n_rc), onehot.reshape(n_rc, GRID_W * GRID_W),
                precision=lax.Precision.HIGHEST).reshape(depth, nh, n_rr, GRID_W, GRID_W)
    t = t * LOG2E + np.where(valid, 0.0, NEG).astype(np.float32)
    t = t.transpose(0, 1, 3, 2, 4)
    t0 = t.reshape(depth, nh, GRID_W, n_rr * GRID_W)
    t1 = jnp.pad(t0[..., GRID_W:], ((0, 0), (0, 0), (0, 0), (0, GRID_W)))
    return jnp.stack([t0, t1], axis=1)


FF_CHUNK = 1024


def _out_ffn_kernel(x_ref, hm_ref, om_ref, yn_ref, mw_ref, wo_ref, nw_ref, w1_ref, w2_ref, fw_ref,
                    o_ref, x1_ref, *, final, tm):
    d_m = hm_ref.shape[1]
    dh = MLSTM_HD
    d_ff = w1_ref.shape[1]
    for t in range(x_ref.shape[0] // tm):
        rs = slice(t * tm, (t + 1) * tm)
        hm = hm_ref[rs, :].astype(F32)
        hn = jnp.concatenate(
            [hm[:, h * dh:(h + 1) * dh]
             * lax.rsqrt(jnp.mean(jnp.square(hm[:, h * dh:(h + 1) * dh]), axis=-1, keepdims=True) + EPS)
             for h in range(d_m // dh)], axis=1) * mw_ref[...]
        ym = (jax.nn.sigmoid(om_ref[rs, :].astype(F32)) * hn).astype(BF16)
        x1_ref[rs, :] = x_ref[rs, :] + _dot(yn_ref[rs, :], wo_ref[d_m:, :]) + _dot(ym, wo_ref[0:d_m, :])
        h = _rms(x1_ref[rs, :], nw_ref[...]).astype(BF16)
        ffn = None
        for j in range(d_ff // FF_CHUNK):
            cs = slice(j * FF_CHUNK, (j + 1) * FF_CHUNK)
            hid = jnp.square(jnp.maximum(_dot(h, w1_ref[:, cs]), 0.0)).astype(BF16)
            part = _dot(hid, w2_ref[cs, :])
            ffn = part if ffn is None else ffn + part
        acc = x1_ref[rs, :] + ffn
        if final:
            acc = _rms(acc, fw_ref[...])
        o_ref[rs, :] = acc


def _out_ffn(x2, hm, om, yn, mw, wo, nw, w1, w2, fw, *, layer, final, tm):
    n, d = x2.shape
    d_m, d_n = hm.shape[1], yn.shape[1]
    tb = SUBTILES * tm
    kern = functools.partial(_out_ffn_kernel, final=final, tm=tm)
    layer_blk = lambda a: pl.BlockSpec((None,) + a.shape[1:], lambda i: (layer, 0, 0),
                                       pipeline_mode=pl.Buffered(1))
    return pl.pallas_call(
        kern,
        grid=(n // tb,),
        in_specs=[
            pl.BlockSpec((tb, d), lambda i: (i, 0)),
            pl.BlockSpec((tb, d_m), lambda i: (i, 0)),
            pl.BlockSpec((tb, d_m), lambda i: (i, 0)),
            pl.BlockSpec((tb, d_n), lambda i: (i, 0)),
            layer_blk(mw), layer_blk(wo), layer_blk(nw), layer_blk(w1), layer_blk(w2),
            pl.BlockSpec(memory_space=pltpu.VMEM),
        ],
        out_specs=pl.BlockSpec((tb, d), lambda i: (i, 0)),
        out_shape=jax.ShapeDtypeStruct((n, d), F32),
        scratch_shapes=[pltpu.VMEM((tb, d), F32)],
        compiler_params=pltpu.CompilerParams(
            dimension_semantics=("arbitrary",), vmem_limit_bytes=VMEM_LIMIT),
        name="out_ffn",
    )(x2, hm, om, yn, mw, wo, nw, w1, w2, fw)


def kernel(x, norm1_w, w_in, conv_w, conv_b, gate_b, mlstm_norm_w, rpb, w_out, norm2_w, w_ff1, w_ff2,
           final_norm_w):
    bsz, seq, d = x.shape
    depth = w_in.shape[0]
    d_m = MLSTM_HEADS * MLSTM_HD
    d_n = NA_HEADS * NA_HD
    n = bsz * seq
    rows = seq // GRID_W
    nch = seq // LCH
    assert w_in.shape[2] == 4 * d_m + N_GATES + 3 * d_n
    assert seq % LCH == 0 and rows % NA_ROWS_PER_STEP == 0 and rows >= NA_KH

    g0 = 4 * d_m
    n0 = g0 + N_GATES
    w_m = w_in[:, :, 0:g0].astype(BF16)
    w_n = jnp.concatenate([w_in[:, :, n0:n0 + d_n] * (NA_HD ** -0.5 * LOG2E),
                           w_in[:, :, n0 + d_n:]], axis=2).astype(BF16)
    gate_order = np.array([0, 2, 1, 3])
    w_g = w_in[:, :, g0:n0].reshape(depth, d, 4, MLSTM_HEADS)[:, :, gate_order, :]
    w_g = w_g.reshape(depth, d, N_GATES).transpose(0, 2, 1).astype(BF16)
    gb = gate_b.reshape(depth, 4, MLSTM_HEADS)[:, gate_order, :].reshape(depth, N_GATES, 1)
    gb = jnp.broadcast_to(gb, (depth, N_GATES, LCH))
    w_o, w_1, w_2 = w_out.astype(BF16), w_ff1.astype(BF16), w_ff2.astype(BF16)
    n1, n2 = norm1_w[:, None, :], norm2_w[:, None, :]
    cb, mn = conv_b[:, None, :], mlstm_norm_w[:, None, :]
    th = _natten_bias_tables(rpb)

    x2 = x.reshape(n, d)
    for l in range(depth):
        q, k, kt, va, o, qkv_n, gates = _in_proj(
            x2, n1, w_m, w_n, w_g, gb, conv_w, cb, layer=l, seq=seq, d_m=d_m, tm=512)
        h_m = _mlstm(
            q.reshape(bsz, seq, d_m), k.reshape(bsz, seq, d_m),
            kt.reshape(MLSTM_HEADS, bsz, nch, MLSTM_HD, LCH), va.reshape(bsz, seq, 2 * d_m),
            gates.reshape(bsz, nch, N_GATE_Q, 2 * MLSTM_HEADS, LCH))
        y_n = _natten(qkv_n.reshape(bsz, seq, 3 * d_n), th, layer=l)
        x2 = _out_ffn(
            x2, h_m.reshape(n, d_m), o, y_n.reshape(n, d_n), mn, w_o, n2, w_1, w_2, final_norm_w[None],
            layer=l, final=(l == depth - 1), tm=512)
    return x2.reshape(bsz, seq, d)
```

```python
import functools

import numpy as np
import jax
import jax.numpy as jnp
from jax import lax
from jax.experimental import pallas as pl
from jax.experimental.pallas import tpu as pltpu

EPS = 1e-6
GRID_W = 64
MLSTM_HEADS = 4
MLSTM_HD = 128
NA_HEADS = 8
NA_HD = 64
NA_KH = 8
NA_KW = 16
CONV_K = 3
N_GATES = 4 * MLSTM_HEADS
LCH = 128
OUT_CHUNKS_PER_ITER = 16
NEG = -1e30
LOG2E = 1.4426950408889634

BF16 = jnp.bfloat16
F32 = jnp.float32

VMEM_LIMIT = 56 * 1024 * 1024


def _dot(a, b):
    return jnp.dot(a, b, preferred_element_type=F32)


def _dot_nt(a, b):
    return lax.dot_general(a, b, (((1,), (1,)), ((), ())), preferred_element_type=F32)


def _rms(x, w):
    return x * lax.rsqrt(jnp.mean(x * x, axis=-1, keepdims=True) + EPS) * w


_WA, _G, _LF, _TOT, _ML = range(5)
N_GATE_Q = 5


def _cumsum_lanes(x):
    lane = lax.broadcasted_iota(jnp.int32, x.shape, 1)
    k = 1
    while k < x.shape[1]:
        x = x + jnp.where(lane >= k, pltpu.roll(x, k, axis=1), 0.0)
        k *= 2
    return x


def _in_proj_kernel(x_ref, xp_ref, xn_ref, nw_ref, wm_ref, wn_ref, wg_ref, gb_ref, cw_ref, cb_ref,
                    q_ref, k_ref, kt_ref, va_ref, o_ref, n_ref, g_ref, *, d_m, tm, steps_per_seq):
    i = pl.program_id(0)
    nsub = x_ref.shape[0] // tm
    dh = MLSTM_HD
    nw = nw_ref[...]
    c0 = 2 * d_m
    wb = 2 * dh
    row = lax.broadcasted_iota(jnp.int32, (tm, wb), 0)
    is_bwd = lax.broadcasted_iota(jnp.int32, (2 * MLSTM_HEADS, LCH), 0) >= MLSTM_HEADS
    ones = jnp.ones((tm, dh), BF16)
    wgt = wg_ref[...]
    seq_start = jnp.where(i % steps_per_seq == 0, 0.0, 1.0)
    seq_end = jnp.where(i % steps_per_seq == steps_per_seq - 1, 0.0, 1.0)

    for t in range(nsub):
        r0 = t * tm
        rs = slice(r0, r0 + tm)
        j0 = r0 // LCH
        x_prev = xp_ref[...] if t == 0 else x_ref[r0 - HALO:r0, :]
        x_next = xn_ref[...] if t == nsub - 1 else x_ref[r0 + tm:r0 + tm + HALO, :]
        first = seq_start if t == 0 else 1.0
        last = seq_end if t == nsub - 1 else 1.0
        hn = jnp.concatenate(
            [_rms(x_ref[rs, :], nw), _rms(x_prev, nw), _rms(x_next, nw)], axis=0).astype(BF16)

        def conv_act(pre, c0_):
            cs = slice(c0_, c0_ + wb)
            x = pre[0:tm]
            xm1 = jnp.where(row == 0, pre[tm + HALO - 1:tm + HALO] * first, pltpu.roll(x, 1, axis=0))
            xp1 = jnp.where(row == tm - 1, pre[tm + HALO:tm + HALO + 1] * last, pltpu.roll(x, tm - 1, axis=0))
            y = cw_ref[0:1, cs] * xm1 + cw_ref[1:2, cs] * x + cw_ref[2:3, cs] * xp1 + cb_ref[:, cs]
            return y * jax.nn.sigmoid(y)

        hn_t = hn[0:tm]
        pre_q0 = _dot_nt(hn, wm_ref[0:wb, :])

        gt = _dot_nt(wgt, hn_t)
        for j in range(tm // LCH):
            x = gt[:, j * LCH:(j + 1) * LCH] + gb_ref[...]
            x_f, x_b = x[0:2 * MLSTM_HEADS], x[2 * MLSTM_HEADS:]
            ig = jnp.where(is_bwd, pltpu.roll(x_b, MLSTM_HEADS, axis=0), x_f) * LOG2E
            f = jnp.where(is_bwd, x_b, pltpu.roll(x_f, MLSTM_HEADS, axis=0))
            lf = (jnp.minimum(f, 0.0) - jnp.log1p(jnp.exp(-jnp.abs(f)))) * LOG2E
            tot = jnp.sum(lf, axis=1, keepdims=True)
            csum = _cumsum_lanes(lf)
            b = jnp.where(is_bwd, tot - csum + lf, csum)
            a = tot - b + ig
            ml = jnp.max(a, axis=1, keepdims=True)
            g_ref[j0 + j, _WA] = jnp.exp2(a - ml)
            g_ref[j0 + j, _G] = ig - b
            g_ref[j0 + j, _LF] = lf
            g_ref[j0 + j, _TOT] = jnp.broadcast_to(tot, lf.shape)
            g_ref[j0 + j, _ML] = jnp.broadcast_to(ml, lf.shape)

        for hp in range(MLSTM_HEADS // 2):
            cs = slice(hp * wb, (hp + 1) * wb)
            q_ref[rs, cs] = conv_act(pre_q0 if hp == 0 else _dot_nt(hn, wm_ref[cs, :]), hp * wb)
            ka = conv_act(_dot_nt(hn, wm_ref[d_m + hp * wb:d_m + (hp + 1) * wb, :]), d_m + hp * wb) * (dh ** -0.5)
            k_ref[rs, cs] = ka.astype(BF16)
            v = _dot_nt(hn_t, wm_ref[c0 + hp * wb:c0 + (hp + 1) * wb, :]).astype(BF16)
            for hh in range(2):
                h = 2 * hp + hh
                for j in range(tm // LCH):
                    kt_ref[h, j0 + j] = ka[j * LCH:(j + 1) * LCH, hh * dh:(hh + 1) * dh].T
                va_ref[rs, 2 * h * dh:(2 * h + 1) * dh] = v[:, hh * dh:(hh + 1) * dh]
                va_ref[rs, (2 * h + 1) * dh:(2 * h + 2) * dh] = ones
        o_ref[rs, :] = _dot_nt(hn_t, wm_ref[c0 + d_m:c0 + 2 * d_m, :]).astype(BF16)
        n_ref[rs, :] = _dot_nt(hn_t, wn_ref[...]).astype(BF16)


HALO = 8
SUBTILES = 2


def _in_proj(x2, nw, wm, wn, wg, gb, cw, cb, *, layer, seq, d_m, tm):
    n, d = x2.shape
    d_n3 = wn.shape[1]
    nh, dh = MLSTM_HEADS, MLSTM_HD
    tb = SUBTILES * tm
    assert seq % tb == 0 and tm % LCH == 0
    kern = functools.partial(_in_proj_kernel, d_m=d_m, tm=tm, steps_per_seq=seq // tb)
    layer_blk = lambda a: pl.BlockSpec((None,) + a.shape[1:], lambda i: (layer, 0, 0),
                                       pipeline_mode=pl.Buffered(1))
    hb = tb // HALO
    return pl.pallas_call(
        kern,
        grid=(n // tb,),
        in_specs=[
            pl.BlockSpec((tb, d), lambda i: (i, 0)),
            pl.BlockSpec((HALO, d), lambda i: (jnp.maximum(i * hb - 1, 0), 0)),
            pl.BlockSpec((HALO, d), lambda i: (jnp.minimum((i + 1) * hb, n // HALO - 1), 0)),
            layer_blk(nw), layer_blk(wm), layer_blk(wn), layer_blk(wg), layer_blk(gb),
            layer_blk(cw), layer_blk(cb),
        ],
        out_specs=[
            pl.BlockSpec((tb, d_m), lambda i: (i, 0)),
            pl.BlockSpec((tb, d_m), lambda i: (i, 0)),
            pl.BlockSpec((nh, tb // LCH, dh, LCH), lambda i: (0, i, 0, 0)),
            pl.BlockSpec((tb, 2 * d_m), lambda i: (i, 0)),
            pl.BlockSpec((tb, d_m), lambda i: (i, 0)),
            pl.BlockSpec((tb, d_n3), lambda i: (i, 0)),
            pl.BlockSpec((tb // LCH, N_GATE_Q, 2 * nh, LCH), lambda i: (i, 0, 0, 0)),
        ],
        out_shape=[
            jax.ShapeDtypeStruct((n, d_m), F32),
            jax.ShapeDtypeStruct((n, d_m), BF16),
            jax.ShapeDtypeStruct((nh, n // LCH, dh, LCH), F32),
            jax.ShapeDtypeStruct((n, 2 * d_m), BF16),
            jax.ShapeDtypeStruct((n, d_m), BF16),
            jax.ShapeDtypeStruct((n, d_n3), BF16),
            jax.ShapeDtypeStruct((n // LCH, N_GATE_Q, 2 * nh, LCH), F32),
        ],
        compiler_params=pltpu.CompilerParams(
            dimension_semantics=("arbitrary",), vmem_limit_bytes=VMEM_LIMIT),
        name="in_proj",
    )(x2, x2, x2, nw, wm, wn, wg, gb, cw, cb)


def _mlstm_kernel(q_ref, k_ref, kt_ref, va_ref, g_ref, h_ref, sf, sb, stf, stb, mp, *, nch):
    hd = pl.program_id(1)
    dh = MLSTM_HD
    gate_row = (hd, MLSTM_HEADS + hd)

    def grow(q_i, d, c):
        return g_ref[0, c, q_i, pl.ds(gate_row[d], 1), :]

    stf[...] = jnp.zeros_like(stf)
    stb[...] = jnp.zeros_like(stb)

    def scan_dir(c, m_prev, st, s_out, d):
        r0 = pl.multiple_of(c * LCH, LCH)
        tot = grow(_TOT, d, c)
        ml = grow(_ML, d, c)
        mp[d, pl.ds(c, 1), :] = m_prev
        s_prev = st[...]
        s_out[c] = s_prev.astype(BF16)
        a = (kt_ref[0, 0, c] * grow(_WA, d, c)).astype(BF16)
        u = _dot(a, va_ref[0, pl.ds(r0, LCH), :])
        m_new = jnp.maximum(tot + m_prev, ml)
        s_old = jnp.exp2(tot + m_prev - m_new)
        s_loc = jnp.exp2(ml - m_new)
        s_old2 = jnp.concatenate([s_old, s_old], axis=1)
        s_loc2 = jnp.concatenate([s_loc, s_loc], axis=1)
        st[...] = s_old2 * s_prev + s_loc2 * u
        return m_new

    def scan_body(i, carry):
        m_f, m_b = carry
        m_f = scan_dir(i, m_f, stf, sf, 0)
        m_b = scan_dir(nch - 1 - i, m_b, stb, sb, 1)
        return m_f, m_b

    zero_row = jnp.zeros((1, LCH), F32)
    lax.fori_loop(0, nch, scan_body, (zero_row, zero_row), unroll=8)

    jj = lax.broadcasted_iota(jnp.int32, (LCH, LCH), 0)
    ss = lax.broadcasted_iota(jnp.int32, (LCH, LCH), 1)
    mask_f = ss <= jj
    mask_b = ss >= jj

    dirs = ((mask_f, sf), (mask_b, sb))

    def weights_dir(c, q32, qk, mask, d):
        g_row = grow(_G, d, c)
        lf_row = grow(_LF, d, c)
        m_prev = mp[d, pl.ds(c, 1), :]
        gm = jnp.where(mask, g_row, -jnp.inf)
        mj = jnp.maximum(jnp.max(gm, axis=1, keepdims=True), m_prev)
        bj = jnp.sum(jnp.where(mask, lf_row, 0.0), axis=1, keepdims=True)
        p = (jnp.exp2(gm - mj) * qk).astype(BF16)
        wq = (jnp.exp2(m_prev - mj) * q32).astype(BF16)
        return jnp.concatenate([wq, p], axis=1), jnp.exp2(-bj - mj)

    def out_body(it, carry):
        cs = [it * OUT_CHUNKS_PER_ITER + j for j in range(OUT_CHUNKS_PER_ITER)]
        r0s = [pl.multiple_of(c * LCH, LCH) for c in cs]
        q32 = [q_ref[0, pl.ds(r0, LCH), :] for r0 in r0s]
        qk = [_dot_nt(q.astype(BF16), k_ref[0, pl.ds(r0, LCH), :]) for q, r0 in zip(q32, r0s)]
        wts = [[weights_dir(c, q, s, mask, d) for d, (mask, _) in enumerate(dirs)]
               for c, q, s in zip(cs, q32, qk)]
        for c, r0, wt in zip(cs, r0s, wts):
            vc = va_ref[0, pl.ds(r0, LCH), :]
            h = None
            for (lhs, bound), (_, s_ref) in zip(wt, dirs):
                tot = _dot(lhs, jnp.concatenate([s_ref[c], vc], axis=0))
                hd_ = tot[:, 0:dh] / jnp.maximum(jnp.abs(tot[:, dh:2 * dh]), bound)
                h = hd_ if h is None else h + hd_
            h_ref[0, pl.ds(r0, LCH), :] = h.astype(BF16)
        return carry

    lax.fori_loop(0, nch // OUT_CHUNKS_PER_ITER, out_body, 0)


def _mlstm(q, k, kt, va, gates):
    bsz, seq, d_m = q.shape
    nh, dh = MLSTM_HEADS, MLSTM_HD
    nch = seq // LCH
    kern = functools.partial(_mlstm_kernel, nch=nch)
    seq_blk = lambda width: pl.BlockSpec((1, seq, width), lambda b, h: (b, 0, h))
    return pl.pallas_call(
        kern,
        grid=(bsz, nh),
        in_specs=[
            seq_blk(dh), seq_blk(dh),
            pl.BlockSpec((1, 1, nch, dh, LCH), lambda b, h: (h, b, 0, 0, 0)),
            seq_blk(2 * dh),
            pl.BlockSpec((1, nch, N_GATE_Q, 2 * nh, LCH), lambda b, h: (b, 0, 0, 0, 0)),
        ],
        out_specs=pl.BlockSpec((1, seq, dh), lambda b, h: (b, 0, h)),
        out_shape=jax.ShapeDtypeStruct((bsz, seq, d_m), BF16),
        scratch_shapes=[
            pltpu.VMEM((nch, dh, 2 * dh), BF16),
            pltpu.VMEM((nch, dh, 2 * dh), BF16),
            pltpu.VMEM((dh, 2 * dh), F32),
            pltpu.VMEM((dh, 2 * dh), F32),
            pltpu.VMEM((2, nch, LCH), F32),
        ],
        compiler_params=pltpu.CompilerParams(
            dimension_semantics=("arbitrary", "arbitrary"), vmem_limit_bytes=VMEM_LIMIT),
        name="mlstm",
    )(q, k, kt, va, gates)


NA_ROWS_PER_STEP = 8
NA_ROWS_PER_ITER = 4


def _natten_kernel(q_ref, k_ref, v_ref, th_ref, o_ref, bias_ref, s_scr, e_scr, l_scr, *, rows):
    rb = pl.program_id(1)
    band = NA_KH * GRID_W
    npair = NA_HEADS // 2
    lane = lax.broadcasted_iota(jnp.int32, (GRID_W, 2 * NA_HD), 1)
    lo = lane < NA_HD
    units = [(j, p) for j in range(NA_ROWS_PER_ITER) for p in range(npair)]

    @pl.when((pl.program_id(0) == 0) & (rb == 0))
    def _():
        for h in range(NA_HEADS):
            for d in range(NA_KH):
                first = NA_KH - 1 - d
                par = first % 2
                off = (first - par) * GRID_W
                bias_ref[h // 2, d, (h % 2) * GRID_W:(h % 2 + 1) * GRID_W, :] = (
                    th_ref[par, h, :, off:off + band])

    def rows_body(it, carry):
        q0s, k0s, ds = [], [], []
        for j in range(NA_ROWS_PER_ITER):
            i = it * NA_ROWS_PER_ITER + j
            r = rb * NA_ROWS_PER_STEP + i
            rs = jnp.clip(r - NA_KH // 2, 0, rows - NA_KH)
            ds.append(r - rs)
            q0s.append(pl.multiple_of(i * GRID_W, GRID_W))
            k0s.append(pl.multiple_of(rs * GRID_W, GRID_W))
        for u, (j, p) in enumerate(units):
            cs = slice(p * 2 * NA_HD, (p + 1) * 2 * NA_HD)
            qp = q_ref[0, pl.ds(q0s[j], GRID_W), cs].astype(F32)
            qq = jnp.concatenate([jnp.where(lo, qp, 0.0), jnp.where(lo, 0.0, qp)], axis=0).astype(BF16)
            s_scr[u] = _dot_nt(qq, k_ref[0, pl.ds(k0s[j], band), cs]) + bias_ref[p, ds[j]]
        for u in range(len(units)):
            s = s_scr[u]
            e = jnp.exp2(s - jnp.max(s, axis=-1, keepdims=True))
            l_scr[u] = jnp.broadcast_to(jnp.sum(e, axis=-1, keepdims=True), l_scr.shape[1:])
            e_scr[u] = e.astype(BF16)
        for u, (j, p) in enumerate(units):
            cs = slice(p * 2 * NA_HD, (p + 1) * 2 * NA_HD)
            o2 = _dot(e_scr[u], v_ref[0, pl.ds(k0s[j], band), cs]) / l_scr[u]
            op = jnp.where(lo, o2[0:GRID_W], o2[GRID_W:2 * GRID_W])
            o_ref[0, pl.ds(q0s[j], GRID_W), cs] = op.astype(BF16)
        return carry

    lax.fori_loop(0, NA_ROWS_PER_STEP // NA_ROWS_PER_ITER, rows_body, 0)


def _natten(qkv, th, *, layer):
    bsz, seq, d3 = qkv.shape
    d_n = d3 // 3
    rows = seq // GRID_W
    tq = NA_ROWS_PER_STEP * GRID_W
    n_units = NA_ROWS_PER_ITER * NA_HEADS // 2
    kern = functools.partial(_natten_kernel, rows=rows)
    return pl.pallas_call(
        kern,
        grid=(bsz, rows // NA_ROWS_PER_STEP),
        in_specs=[
            pl.BlockSpec((1, tq, d_n), lambda b, r: (b, r, 0)),
            pl.BlockSpec((1, seq, d_n), lambda b, r: (b, 0, 1)),
            pl.BlockSpec((1, seq, d_n), lambda b, r: (b, 0, 2)),
            pl.BlockSpec((None,) + th.shape[1:], lambda b, r: (layer, 0, 0, 0, 0)),
        ],
        out_specs=pl.BlockSpec((1, tq, d_n), lambda b, r: (b, r, 0)),
        out_shape=jax.ShapeDtypeStruct((bsz, seq, d_n), BF16),
        scratch_shapes=[
            pltpu.VMEM((NA_HEADS // 2, NA_KH, 2 * GRID_W, NA_KH * GRID_W), F32),
            pltpu.VMEM((n_units, 2 * GRID_W, NA_KH * GRID_W), F32),
            pltpu.VMEM((n_units, 2 * GRID_W, NA_KH * GRID_W), BF16),
            pltpu.VMEM((n_units, 2 * GRID_W, 2 * NA_HD), F32),
        ],
        compiler_params=pltpu.CompilerParams(
            dimension_semantics=("arbitrary", "arbitrary"), vmem_limit_bytes=VMEM_LIMIT),
        name="natten",
    )(qkv, qkv, qkv, th)


def _natten_bias_tables(rpb):
    depth, nh = rpb.shape[0], rpb.shape[1]
    n_rr, n_rc = 2 * NA_KH - 1, 2 * NA_KW - 1
    cols = np.arange(GRID_W)
    c_start = np.clip(cols - NA_KW // 2, 0, GRID_W - NA_KW)
    kc = np.arange(GRID_W)
    valid = (kc[None, :] >= c_start[:, None]) & (kc[None, :] < c_start[:, None] + NA_KW)
    rel_c = kc[None, :] - cols[:, None] + NA_KW - 1
    onehot = ((rel_c[None] == np.arange(n_rc)[:, None, None]) & valid[None]).astype(np.float32)
    t = jnp.dot(rpb.astype(F32).reshape(depth * nh * n_rr, n_rc), onehot.reshape(n_rc, GRID_W * GRID_W),
                precision=lax.Precision.HIGHEST).reshape(depth, nh, n_rr, GRID_W, GRID_W)
    t = t * LOG2E + np.where(valid, 0.0, NEG).astype(np.float32)
    t = t.transpose(0, 1, 3, 2, 4)
    t0 = t.reshape(depth, nh, GRID_W, n_rr * GRID_W)
    t1 = jnp.pad(t0[..., GRID_W:], ((0, 0), (0, 0), (0, 0), (0, GRID_W)))
    return jnp.stack([t0, t1], axis=1)


FF_CHUNK = 1024


def _out_ffn_kernel(x_ref, hm_ref, om_ref, yn_ref, mw_ref, wo_ref, nw_ref, w1_ref, w2_ref, fw_ref,
                    o_ref, x1_ref, *, final, tm):
    d_m = hm_ref.shape[1]
    dh = MLSTM_HD
    d_ff = w1_ref.shape[1]
    for t in range(x_ref.shape[0] // tm):
        rs = slice(t * tm, (t + 1) * tm)
        hm = hm_ref[rs, :].astype(F32)
        hn = jnp.concatenate(
            [hm[:, h * dh:(h + 1) * dh]
             * lax.rsqrt(jnp.mean(jnp.square(hm[:, h * dh:(h + 1) * dh]), axis=-1, keepdims=True) + EPS)
             for h in range(d_m // dh)], axis=1) * mw_ref[...]
        ym = (jax.nn.sigmoid(om_ref[rs, :].astype(F32)) * hn).astype(BF16)
        x1_ref[rs, :] = x_ref[rs, :] + _dot(yn_ref[rs, :], wo_ref[d_m:, :]) + _dot(ym, wo_ref[0:d_m, :])
        h = _rms(x1_ref[rs, :], nw_ref[...]).astype(BF16)
        ffn = None
        for j in range(d_ff // FF_CHUNK):
            cs = slice(j * FF_CHUNK, (j + 1) * FF_CHUNK)
            hid = jnp.square(jnp.maximum(_dot(h, w1_ref[:, cs]), 0.0)).astype(BF16)
            part = _dot(hid, w2_ref[cs, :])
            ffn = part if ffn is None else ffn + part
        acc = x1_ref[rs, :] + ffn
        if final:
            acc = _rms(acc, fw_ref[...])
        o_ref[rs, :] = acc


def _out_ffn(x2, hm, om, yn, mw, wo, nw, w1, w2, fw, *, layer, final, tm):
    n, d = x2.shape
    d_m, d_n = hm.shape[1], yn.shape[1]
    tb = SUBTILES * tm
    kern = functools.partial(_out_ffn_kernel, final=final, tm=tm)
    layer_blk = lambda a: pl.BlockSpec((None,) + a.shape[1:], lambda i: (layer, 0, 0),
                                       pipeline_mode=pl.Buffered(1))
    return pl.pallas_call(
        kern,
        grid=(n // tb,),
        in_specs=[
            pl.BlockSpec((tb, d), lambda i: (i, 0)),
            pl.BlockSpec((tb, d_m), lambda i: (i, 0)),
            pl.BlockSpec((tb, d_m), lambda i: (i, 0)),
            pl.BlockSpec((tb, d_n), lambda i: (i, 0)),
            layer_blk(mw), layer_blk(wo), layer_blk(nw), layer_blk(w1), layer_blk(w2),
            pl.BlockSpec(memory_space=pltpu.VMEM),
        ],
        out_specs=pl.BlockSpec((tb, d), lambda i: (i, 0)),
        out_shape=jax.ShapeDtypeStruct((n, d), F32),
        scratch_shapes=[pltpu.VMEM((tb, d), F32)],
        compiler_params=pltpu.CompilerParams(
            dimension_semantics=("arbitrary",), vmem_limit_bytes=VMEM_LIMIT),
        name="out_ffn",
    )(x2, hm, om, yn, mw, wo, nw, w1, w2, fw)


def kernel(x, norm1_w, w_in, conv_w, conv_b, gate_b, mlstm_norm_w, rpb, w_out, norm2_w, w_ff1, w_ff2,
           final_norm_w):
    bsz, seq, d = x.shape
    depth = w_in.shape[0]
    d_m = MLSTM_HEADS * MLSTM_HD
    d_n = NA_HEADS * NA_HD
    n = bsz * seq
    rows = seq // GRID_W
    nch = seq // LCH
    assert w_in.shape[2] == 4 * d_m + N_GATES + 3 * d_n
    assert seq % LCH == 0 and rows % NA_ROWS_PER_STEP == 0 and rows >= NA_KH

    g0 = 4 * d_m
    n0 = g0 + N_GATES
    w_t = jnp.swapaxes(w_in, 1, 2)
    w_m = w_t[:, 0:g0].astype(BF16)
    w_g = w_t[:, g0:n0].astype(BF16)
    w_n = jnp.concatenate([w_t[:, n0:n0 + d_n] * (NA_HD ** -0.5 * LOG2E),
                           w_t[:, n0 + d_n:]], axis=1).astype(BF16)
    gb = jnp.broadcast_to(gate_b[:, :, None], (depth, N_GATES, LCH))
    w_o, w_1, w_2 = w_out.astype(BF16), w_ff1.astype(BF16), w_ff2.astype(BF16)
    n1, n2 = norm1_w[:, None, :], norm2_w[:, None, :]
    cb, mn = conv_b[:, None, :], mlstm_norm_w[:, None, :]
    th = _natten_bias_tables(rpb)

    x2 = x.reshape(n, d)
    for l in range(depth):
        q, k, kt, va, o, qkv_n, gates = _in_proj(
            x2, n1, w_m, w_n, w_g, gb, conv_w, cb, layer=l, seq=seq, d_m=d_m, tm=512)
        h_m = _mlstm(
            q.reshape(bsz, seq, d_m), k.reshape(bsz, seq, d_m),
            kt.reshape(MLSTM_HEADS, bsz, nch, MLSTM_HD, LCH), va.reshape(bsz, seq, 2 * d_m),
            gates.reshape(bsz, nch, N_GATE_Q, 2 * MLSTM_HEADS, LCH))
        y_n = _natten(qkv_n.reshape(bsz, seq, 3 * d_n), th, layer=l)
        x2 = _out_ffn(
            x2, h_m.reshape(n, d_m), o, y_n.reshape(n, d_n), mn, w_o, n2, w_1, w_2, final_norm_w[None],
            layer=l, final=(l == depth - 1), tm=512)
    return x2.reshape(bsz, seq, d)
```

```python
import functools

import numpy as np
import jax
import jax.numpy as jnp
from jax import lax
from jax.experimental import pallas as pl
from jax.experimental.pallas import tpu as pltpu

EPS = 1e-6
GRID_W = 64
MLSTM_HEADS = 4
MLSTM_HD = 128
NA_HEADS = 8
NA_HD = 64
NA_KH = 8
NA_KW = 16
CONV_K = 3
N_GATES = 4 * MLSTM_HEADS
LCH = 128
OUT_CHUNKS_PER_ITER = 16
NEG = -1e30
LOG2E = 1.4426950408889634

BF16 = jnp.bfloat16
F32 = jnp.float32

VMEM_LIMIT = 56 * 1024 * 1024


def _dot(a, b):
    return jnp.dot(a, b, preferred_element_type=F32)


def _dot_nt(a, b):
    return lax.dot_general(a, b, (((1,), (1,)), ((), ())), preferred_element_type=F32)


def _rms(x, w):
    return x * lax.rsqrt(jnp.mean(x * x, axis=-1, keepdims=True) + EPS) * w


_WA, _G, _LF, _TOT, _ML = range(5)
N_GATE_Q = 5


def _cumsum_lanes(x):
    lane = lax.broadcasted_iota(jnp.int32, x.shape, 1)
    k = 1
    while k < x.shape[1]:
        x = x + jnp.where(lane >= k, pltpu.roll(x, k, axis=1), 0.0)
        k *= 2
    return x


def _in_proj_kernel(x_ref, xp_ref, xn_ref, nw_ref, wm_ref, wn_ref, wg_ref, gb_ref, cw_ref, cb_ref,
                    q_ref, k_ref, kt_ref, va_ref, o_ref, n_ref, g_ref, *, d_m, tm, steps_per_seq):
    i = pl.program_id(0)
    nsub = x_ref.shape[0] // tm
    dh = MLSTM_HD
    nw = nw_ref[...]
    c0 = 2 * d_m
    wb = 2 * dh
    row = lax.broadcasted_iota(jnp.int32, (tm, wb), 0)
    is_bwd = lax.broadcasted_iota(jnp.int32, (2 * MLSTM_HEADS, LCH), 0) >= MLSTM_HEADS
    ones = jnp.ones((tm, dh), BF16)
    wgt = wg_ref[...]
    seq_start = jnp.where(i % steps_per_seq == 0, 0.0, 1.0)
    seq_end = jnp.where(i % steps_per_seq == steps_per_seq - 1, 0.0, 1.0)

    for t in range(nsub):
        r0 = t * tm
        rs = slice(r0, r0 + tm)
        j0 = r0 // LCH
        x_prev = xp_ref[...] if t == 0 else x_ref[r0 - HALO:r0, :]
        x_next = xn_ref[...] if t == nsub - 1 else x_ref[r0 + tm:r0 + tm + HALO, :]
        first = seq_start if t == 0 else 1.0
        last = seq_end if t == nsub - 1 else 1.0
        hn = jnp.concatenate(
            [_rms(x_ref[rs, :], nw), _rms(x_prev, nw), _rms(x_next, nw)], axis=0).astype(BF16)

        def conv_act(pre, c0_):
            cs = slice(c0_, c0_ + wb)
            x = pre[0:tm]
            xm1 = jnp.where(row == 0, pre[tm + HALO - 1:tm + HALO] * first, pltpu.roll(x, 1, axis=0))
            xp1 = jnp.where(row == tm - 1, pre[tm + HALO:tm + HALO + 1] * last, pltpu.roll(x, tm - 1, axis=0))
            y = cw_ref[0:1, cs] * xm1 + cw_ref[1:2, cs] * x + cw_ref[2:3, cs] * xp1 + cb_ref[:, cs]
            return y * jax.nn.sigmoid(y)

        hn_t = hn[0:tm]
        pre_q0 = _dot_nt(hn, wm_ref[0:wb, :])

        gt = _dot_nt(wgt, hn_t)
        for j in range(tm // LCH):
            x = gt[:, j * LCH:(j + 1) * LCH] + gb_ref[...]
            x_f, x_b = x[0:2 * MLSTM_HEADS], x[2 * MLSTM_HEADS:]
            ig = jnp.where(is_bwd, pltpu.roll(x_b, MLSTM_HEADS, axis=0), x_f) * LOG2E
            f = jnp.where(is_bwd, x_b, pltpu.roll(x_f, MLSTM_HEADS, axis=0))
            lf = (jnp.minimum(f, 0.0) - jnp.log1p(jnp.exp(-jnp.abs(f)))) * LOG2E
            tot = jnp.sum(lf, axis=1, keepdims=True)
            csum = _cumsum_lanes(lf)
            b = jnp.where(is_bwd, tot - csum + lf, csum)
            a = tot - b + ig
            ml = jnp.max(a, axis=1, keepdims=True)
            g_ref[j0 + j, _WA] = jnp.exp2(a - ml)
            g_ref[j0 + j, _G] = ig - b
            g_ref[j0 + j, _LF] = lf
            g_ref[j0 + j, _TOT] = jnp.broadcast_to(tot, lf.shape)
            g_ref[j0 + j, _ML] = jnp.broadcast_to(ml, lf.shape)

        for hp in range(MLSTM_HEADS // 2):
            cs = slice(hp * wb, (hp + 1) * wb)
            q_ref[rs, cs] = conv_act(pre_q0 if hp == 0 else _dot_nt(hn, wm_ref[cs, :]), hp * wb)
            ka = conv_act(_dot_nt(hn, wm_ref[d_m + hp * wb:d_m + (hp + 1) * wb, :]), d_m + hp * wb) * (dh ** -0.5)
            k_ref[rs, cs] = ka.astype(BF16)
            v = _dot_nt(hn_t, wm_ref[c0 + hp * wb:c0 + (hp + 1) * wb, :]).astype(BF16)
            for hh in range(2):
                h = 2 * hp + hh
                for j in range(tm // LCH):
                    kt_ref[h, j0 + j] = ka[j * LCH:(j + 1) * LCH, hh * dh:(hh + 1) * dh].T
                va_ref[rs, 2 * h * dh:(2 * h + 1) * dh] = v[:, hh * dh:(hh + 1) * dh]
                va_ref[rs, (2 * h + 1) * dh:(2 * h + 2) * dh] = ones
        o_ref[rs, :] = _dot_nt(hn_t, wm_ref[c0 + d_m:c0 + 2 * d_m, :]).astype(BF16)
        n_ref[rs, :] = _dot_nt(hn_t, wn_ref[...]).astype(BF16)


HALO = 8
SUBTILES = 2


def _in_proj(x2, nw, wm, wn, wg, gb, cw, cb, *, layer, seq, d_m, tm):
    n, d = x2.shape
    d_n3 = wn.shape[1]
    nh, dh = MLSTM_HEADS, MLSTM_HD
    tb = SUBTILES * tm
    assert seq % tb == 0 and tm % LCH == 0
    kern = functools.partial(_in_proj_kernel, d_m=d_m, tm=tm, steps_per_seq=seq // tb)
    layer_blk = lambda a: pl.BlockSpec((None,) + a.shape[1:], lambda i: (layer, 0, 0),
                                       pipeline_mode=pl.Buffered(1))
    hb = tb // HALO
    return pl.pallas_call(
        kern,
        grid=(n // tb,),
        in_specs=[
            pl.BlockSpec((tb, d), lambda i: (i, 0)),
            pl.BlockSpec((HALO, d), lambda i: (jnp.maximum(i * hb - 1, 0), 0)),
            pl.BlockSpec((HALO, d), lambda i: (jnp.minimum((i + 1) * hb, n // HALO - 1), 0)),
            layer_blk(nw), layer_blk(wm), layer_blk(wn), layer_blk(wg), layer_blk(gb),
            layer_blk(cw), layer_blk(cb),
        ],
        out_specs=[
            pl.BlockSpec((tb, d_m), lambda i: (i, 0)),
            pl.BlockSpec((tb, d_m), lambda i: (i, 0)),
            pl.BlockSpec((nh, tb // LCH, dh, LCH), lambda i: (0, i, 0, 0)),
            pl.BlockSpec((tb, 2 * d_m), lambda i: (i, 0)),
            pl.BlockSpec((tb, d_m), lambda i: (i, 0)),
            pl.BlockSpec((tb, d_n3), lambda i: (i, 0)),
            pl.BlockSpec((tb // LCH, N_GATE_Q, 2 * nh, LCH), lambda i: (i, 0, 0, 0)),
        ],
        out_shape=[
            jax.ShapeDtypeStruct((n, d_m), F32),
            jax.ShapeDtypeStruct((n, d_m), BF16),
            jax.ShapeDtypeStruct((nh, n // LCH, dh, LCH), F32),
            jax.ShapeDtypeStruct((n, 2 * d_m), BF16),
            jax.ShapeDtypeStruct((n, d_m), BF16),
            jax.ShapeDtypeStruct((n, d_n3), BF16),
            jax.ShapeDtypeStruct((n // LCH, N_GATE_Q, 2 * nh, LCH), F32),
        ],
        compiler_params=pltpu.CompilerParams(
            dimension_semantics=("arbitrary",), vmem_limit_bytes=VMEM_LIMIT),
        name="in_proj",
    )(x2, x2, x2, nw, wm, wn, wg, gb, cw, cb)


def _mlstm_kernel(q_ref, k_ref, kt_ref, va_ref, g_ref, h_ref, sf, sb, stf, stb, mp, *, nch):
    hd = pl.program_id(1)
    dh = MLSTM_HD
    gate_row = (hd, MLSTM_HEADS + hd)

    def grow(q_i, d, c):
        return g_ref[0, c, q_i, pl.ds(gate_row[d], 1), :]

    stf[...] = jnp.zeros_like(stf)
    stb[...] = jnp.zeros_like(stb)

    def scan_dir(c, m_prev, st, s_out, d):
        r0 = pl.multiple_of(c * LCH, LCH)
        tot = grow(_TOT, d, c)
        ml = grow(_ML, d, c)
        mp[d, pl.ds(c, 1), :] = m_prev
        s_prev = st[...]
        s_out[c] = s_prev.astype(BF16)
        a = (kt_ref[0, 0, c] * grow(_WA, d, c)).astype(BF16)
        u = _dot(a, va_ref[0, pl.ds(r0, LCH), :])
        m_new = jnp.maximum(tot + m_prev, ml)
        s_old = jnp.exp2(tot + m_prev - m_new)
        s_loc = jnp.exp2(ml - m_new)
        s_old2 = jnp.concatenate([s_old, s_old], axis=1)
        s_loc2 = jnp.concatenate([s_loc, s_loc], axis=1)
        st[...] = s_old2 * s_prev + s_loc2 * u
        return m_new

    def scan_body(i, carry):
        m_f, m_b = carry
        m_f = scan_dir(i, m_f, stf, sf, 0)
        m_b = scan_dir(nch - 1 - i, m_b, stb, sb, 1)
        return m_f, m_b

    zero_row = jnp.zeros((1, LCH), F32)
    lax.fori_loop(0, nch, scan_body, (zero_row, zero_row), unroll=8)

    jj = lax.broadcasted_iota(jnp.int32, (LCH, LCH), 0)
    ss = lax.broadcasted_iota(jnp.int32, (LCH, LCH), 1)
    mask_f = ss <= jj
    mask_b = ss >= jj

    dirs = ((mask_f, sf), (mask_b, sb))

    def weights_dir(c, q32, qk, mask, d):
        g_row = grow(_G, d, c)
        lf_row = grow(_LF, d, c)
        m_prev = mp[d, pl.ds(c, 1), :]
        gm = jnp.where(mask, g_row, -jnp.inf)
        mj = jnp.maximum(jnp.max(gm, axis=1, keepdims=True), m_prev)
        bj = jnp.sum(jnp.where(mask, lf_row, 0.0), axis=1, keepdims=True)
        p = (jnp.exp2(gm - mj) * qk).astype(BF16)
        wq = (jnp.exp2(m_prev - mj) * q32).astype(BF16)
        return jnp.concatenate([wq, p], axis=1), jnp.exp2(-bj - mj)

    def out_body(it, carry):
        cs = [it * OUT_CHUNKS_PER_ITER + j for j in range(OUT_CHUNKS_PER_ITER)]
        r0s = [pl.multiple_of(c * LCH, LCH) for c in cs]
        q32 = [q_ref[0, pl.ds(r0, LCH), :] for r0 in r0s]
        qk = [_dot_nt(q.astype(BF16), k_ref[0, pl.ds(r0, LCH), :]) for q, r0 in zip(q32, r0s)]
        wts = [[weights_dir(c, q, s, mask, d) for d, (mask, _) in enumerate(dirs)]
               for c, q, s in zip(cs, q32, qk)]
        for c, r0, wt in zip(cs, r0s, wts):
            vc = va_ref[0, pl.ds(r0, LCH), :]
            h = None
            for (lhs, bound), (_, s_ref) in zip(wt, dirs):
                tot = _dot(lhs, jnp.concatenate([s_ref[c], vc], axis=0))
                hd_ = tot[:, 0:dh] / jnp.maximum(jnp.abs(tot[:, dh:2 * dh]), bound)
                h = hd_ if h is None else h + hd_
            h_ref[0, pl.ds(r0, LCH), :] = h.astype(BF16)
        return carry

    lax.fori_loop(0, nch // OUT_CHUNKS_PER_ITER, out_body, 0)


def _mlstm(q, k, kt, va, gates):
    bsz, seq, d_m = q.shape
    nh, dh = MLSTM_HEADS, MLSTM_HD
    nch = seq // LCH
    kern = functools.partial(_mlstm_kernel, nch=nch)
    seq_blk = lambda width: pl.BlockSpec((1, seq, width), lambda b, h: (b, 0, h))
    return pl.pallas_call(
        kern,
        grid=(bsz, nh),
        in_specs=[
            seq_blk(dh), seq_blk(dh),
            pl.BlockSpec((1, 1, nch, dh, LCH), lambda b, h: (h, b, 0, 0, 0)),
            seq_blk(2 * dh),
            pl.BlockSpec((1, nch, N_GATE_Q, 2 * nh, LCH), lambda b, h: (b, 0, 0, 0, 0)),
        ],
        out_specs=pl.BlockSpec((1, seq, dh), lambda b, h: (b, 0, h)),
        out_shape=jax.ShapeDtypeStruct((bsz, seq, d_m), BF16),
        scratch_shapes=[
            pltpu.VMEM((nch, dh, 2 * dh), BF16),
            pltpu.VMEM((nch, dh, 2 * dh), BF16),
            pltpu.VMEM((dh, 2 * dh), F32),
            pltpu.VMEM((dh, 2 * dh), F32),
            pltpu.VMEM((2, nch, LCH), F32),
        ],
        compiler_params=pltpu.CompilerParams(
            dimension_semantics=("arbitrary", "arbitrary"), vmem_limit_bytes=VMEM_LIMIT),
        name="mlstm",
    )(q, k, kt, va, gates)


NA_ROWS_PER_STEP = 16
NA_ROWS_PER_ITER = 4


def _natten_kernel(q_ref, k_ref, v_ref, rpb_ref, o_ref, bias_ref, s_scr, e_scr, l_scr, *, rows):
    rb = pl.program_id(1)
    band = NA_KH * GRID_W
    npair = NA_HEADS // 2
    lane = lax.broadcasted_iota(jnp.int32, (GRID_W, 2 * NA_HD), 1)
    lo = lane < NA_HD
    units = [(j, p) for j in range(NA_ROWS_PER_ITER) for p in range(npair)]

    @pl.when((pl.program_id(0) == 0) & (rb == 0))
    def _():
        n_rr = 2 * NA_KH - 1
        col = lax.broadcasted_iota(jnp.int32, (GRID_W, 2 * GRID_W), 0)
        kcl = lax.broadcasted_iota(jnp.int32, (GRID_W, 2 * GRID_W), 1)
        c_start = jnp.clip(col - NA_KW // 2, 0, GRID_W - NA_KW)
        valid = (kcl >= c_start) & (kcl < c_start + NA_KW)
        for h in range(NA_HEADS):
            toep = [jnp.where(valid,
                              pltpu.roll(jnp.broadcast_to(rpb_ref[h, rr:rr + 1, :], col.shape),
                                         2 * GRID_W - (NA_KW - 1), 1, stride=1, stride_axis=0),
                              NEG)
                    for rr in range(n_rr)]
            pair = [jnp.where(kcl < GRID_W, toep[rr], pltpu.roll(toep[rr + 1], GRID_W, 1))
                    for rr in range(n_rr - 1)]
            for d in range(NA_KH):
                for m in range(NA_KH // 2):
                    bias_ref[h // 2, d, (h % 2) * GRID_W:(h % 2 + 1) * GRID_W,
                             2 * m * GRID_W:2 * (m + 1) * GRID_W] = pair[NA_KH - 1 - d + 2 * m]

    def rows_body(it, carry):
        q0s, k0s, ds = [], [], []
        for j in range(NA_ROWS_PER_ITER):
            i = it * NA_ROWS_PER_ITER + j
            r = rb * NA_ROWS_PER_STEP + i
            rs = jnp.clip(r - NA_KH // 2, 0, rows - NA_KH)
            ds.append(r - rs)
            q0s.append(pl.multiple_of(i * GRID_W, GRID_W))
            k0s.append(pl.multiple_of(rs * GRID_W, GRID_W))
        for u, (j, p) in enumerate(units):
            cs = slice(p * 2 * NA_HD, (p + 1) * 2 * NA_HD)
            qp = q_ref[0, pl.ds(q0s[j], GRID_W), cs].astype(F32)
            qq = jnp.concatenate([jnp.where(lo, qp, 0.0), jnp.where(lo, 0.0, qp)], axis=0).astype(BF16)
            s_scr[u] = _dot_nt(qq, k_ref[0, pl.ds(k0s[j], band), cs]) + bias_ref[p, ds[j]]
        for u in range(len(units)):
            s = s_scr[u]
            e = jnp.exp2(s - jnp.max(s, axis=-1, keepdims=True))
            l_scr[u] = jnp.broadcast_to(jnp.sum(e, axis=-1, keepdims=True), l_scr.shape[1:])
            e_scr[u] = e.astype(BF16)
        for u, (j, p) in enumerate(units):
            cs = slice(p * 2 * NA_HD, (p + 1) * 2 * NA_HD)
            o2 = _dot(e_scr[u], v_ref[0, pl.ds(k0s[j], band), cs]) / l_scr[u]
            op = jnp.where(lo, o2[0:GRID_W], o2[GRID_W:2 * GRID_W])
            o_ref[0, pl.ds(q0s[j], GRID_W), cs] = op.astype(BF16)
        return carry

    lax.fori_loop(0, NA_ROWS_PER_STEP // NA_ROWS_PER_ITER, rows_body, 0)


def _natten(qkv, rpb_pad, *, layer):
    bsz, seq, d3 = qkv.shape
    d_n = d3 // 3
    rows = seq // GRID_W
    tq = NA_ROWS_PER_STEP * GRID_W
    n_units = NA_ROWS_PER_ITER * NA_HEADS // 2
    kern = functools.partial(_natten_kernel, rows=rows)
    return pl.pallas_call(
        kern,
        grid=(bsz, rows // NA_ROWS_PER_STEP),
        in_specs=[
            pl.BlockSpec((1, tq, d_n), lambda b, r: (b, r, 0)),
            pl.BlockSpec((1, seq, d_n), lambda b, r: (b, 0, 1)),
            pl.BlockSpec((1, seq, d_n), lambda b, r: (b, 0, 2)),
            pl.BlockSpec((None,) + rpb_pad.shape[1:], lambda b, r: (layer, 0, 0, 0)),
        ],
        out_specs=pl.BlockSpec((1, tq, d_n), lambda b, r: (b, r, 0)),
        out_shape=jax.ShapeDtypeStruct((bsz, seq, d_n), BF16),
        scratch_shapes=[
            pltpu.VMEM((NA_HEADS // 2, NA_KH, 2 * GRID_W, NA_KH * GRID_W), F32),
            pltpu.VMEM((n_units, 2 * GRID_W, NA_KH * GRID_W), F32),
            pltpu.VMEM((n_units, 2 * GRID_W, NA_KH * GRID_W), BF16),
            pltpu.VMEM((n_units, 2 * GRID_W, 2 * NA_HD), F32),
        ],
        compiler_params=pltpu.CompilerParams(
            dimension_semantics=("arbitrary", "arbitrary"), vmem_limit_bytes=VMEM_LIMIT),
        name="natten",
    )(qkv, qkv, qkv, rpb_pad)


def _pad_rpb(rpb):
    n_rr, n_rc = rpb.shape[2], rpb.shape[3]
    return jnp.pad(rpb.astype(F32) * LOG2E, ((0, 0), (0, 0), (0, 2 * NA_KH - n_rr), (0, 2 * GRID_W - n_rc)))


FF_CHUNK = 1024


def _out_ffn_kernel(x_ref, hm_ref, om_ref, yn_ref, mw_ref, wo_ref, nw_ref, w1_ref, w2_ref, fw_ref,
                    o_ref, x1_ref, *, final, tm):
    d_m = hm_ref.shape[1]
    dh = MLSTM_HD
    d_ff = w1_ref.shape[1]
    for t in range(x_ref.shape[0] // tm):
        rs = slice(t * tm, (t + 1) * tm)
        hm = hm_ref[rs, :].astype(F32)
        hn = jnp.concatenate(
            [hm[:, h * dh:(h + 1) * dh]
             * lax.rsqrt(jnp.mean(jnp.square(hm[:, h * dh:(h + 1) * dh]), axis=-1, keepdims=True) + EPS)
             for h in range(d_m // dh)], axis=1) * mw_ref[...]
        ym = (jax.nn.sigmoid(om_ref[rs, :].astype(F32)) * hn).astype(BF16)
        x1_ref[rs, :] = x_ref[rs, :] + _dot(yn_ref[rs, :], wo_ref[d_m:, :]) + _dot(ym, wo_ref[0:d_m, :])
        h = _rms(x1_ref[rs, :], nw_ref[...]).astype(BF16)
        ffn = None
        for j in range(d_ff // FF_CHUNK):
            cs = slice(j * FF_CHUNK, (j + 1) * FF_CHUNK)
            hid = jnp.square(jnp.maximum(_dot(h, w1_ref[:, cs]), 0.0)).astype(BF16)
            part = _dot(hid, w2_ref[cs, :])
            ffn = part if ffn is None else ffn + part
        acc = x1_ref[rs, :] + ffn
        if final:
            acc = _rms(acc, fw_ref[...])
        o_ref[rs, :] = acc


def _out_ffn(x2, hm, om, yn, mw, wo, nw, w1, w2, fw, *, layer, final, tm):
    n, d = x2.shape
    d_m, d_n = hm.shape[1], yn.shape[1]
    tb = SUBTILES * tm
    kern = functools.partial(_out_ffn_kernel, final=final, tm=tm)
    layer_blk = lambda a: pl.BlockSpec((None,) + a.shape[1:], lambda i: (layer, 0, 0),
                                       pipeline_mode=pl.Buffered(1))
    return pl.pallas_call(
        kern,
        grid=(n // tb,),
        in_specs=[
            pl.BlockSpec((tb, d), lambda i: (i, 0)),
            pl.BlockSpec((tb, d_m), lambda i: (i, 0)),
            pl.BlockSpec((tb, d_m), lambda i: (i, 0)),
            pl.BlockSpec((tb, d_n), lambda i: (i, 0)),
            layer_blk(mw), layer_blk(wo), layer_blk(nw), layer_blk(w1), layer_blk(w2),
            pl.BlockSpec(memory_space=pltpu.VMEM),
        ],
        out_specs=pl.BlockSpec((tb, d), lambda i: (i, 0)),
        out_shape=jax.ShapeDtypeStruct((n, d), F32),
        scratch_shapes=[pltpu.VMEM((tb, d), F32)],
        compiler_params=pltpu.CompilerParams(
            dimension_semantics=("arbitrary",), vmem_limit_bytes=VMEM_LIMIT),
        name="out_ffn",
    )(x2, hm, om, yn, mw, wo, nw, w1, w2, fw)


def kernel(x, norm1_w, w_in, conv_w, conv_b, gate_b, mlstm_norm_w, rpb, w_out, norm2_w, w_ff1, w_ff2,
           final_norm_w):
    bsz, seq, d = x.shape
    depth = w_in.shape[0]
    d_m = MLSTM_HEADS * MLSTM_HD
    d_n = NA_HEADS * NA_HD
    n = bsz * seq
    rows = seq // GRID_W
    nch = seq // LCH
    assert w_in.shape[2] == 4 * d_m + N_GATES + 3 * d_n
    assert seq % LCH == 0 and rows % NA_ROWS_PER_STEP == 0 and rows >= NA_KH

    g0 = 4 * d_m
    n0 = g0 + N_GATES
    w_t = jnp.swapaxes(w_in, 1, 2)
    w_m = w_t[:, 0:g0].astype(BF16)
    w_g = w_t[:, g0:n0].astype(BF16)
    w_n = jnp.concatenate([w_t[:, n0:n0 + d_n] * (NA_HD ** -0.5 * LOG2E),
                           w_t[:, n0 + d_n:]], axis=1).astype(BF16)
    gb = jnp.broadcast_to(gate_b[:, :, None], (depth, N_GATES, LCH))
    w_o, w_1, w_2 = w_out.astype(BF16), w_ff1.astype(BF16), w_ff2.astype(BF16)
    n1, n2 = norm1_w[:, None, :], norm2_w[:, None, :]
    cb, mn = conv_b[:, None, :], mlstm_norm_w[:, None, :]
    rpb_pad = _pad_rpb(rpb)

    x2 = x.reshape(n, d)
    for l in range(depth):
        q, k, kt, va, o, qkv_n, gates = _in_proj(
            x2, n1, w_m, w_n, w_g, gb, conv_w, cb, layer=l, seq=seq, d_m=d_m, tm=512)
        h_m = _mlstm(
            q.reshape(bsz, seq, d_m), k.reshape(bsz, seq, d_m),
            kt.reshape(MLSTM_HEADS, bsz, nch, MLSTM_HD, LCH), va.reshape(bsz, seq, 2 * d_m),
            gates.reshape(bsz, nch, N_GATE_Q, 2 * MLSTM_HEADS, LCH))
        y_n = _natten(qkv_n.reshape(bsz, seq, 3 * d_n), rpb_pad, layer=l)
        x2 = _out_ffn(
            x2, h_m.reshape(n, d_m), o, y_n.reshape(n, d_n), mn, w_o, n2, w_1, w_2, final_norm_w[None],
            layer=l, final=(l == depth - 1), tm=512)
    return x2.reshape(bsz, seq, d)
```

```python
import functools

import numpy as np
import jax
import jax.numpy as jnp
from jax import lax
from jax.experimental import pallas as pl
from jax.experimental.pallas import tpu as pltpu

EPS = 1e-6
GRID_W = 64
MLSTM_HEADS = 4
MLSTM_HD = 128
NA_HEADS = 8
NA_HD = 64
NA_KH = 8
NA_KW = 16
CONV_K = 3
N_GATES = 4 * MLSTM_HEADS
LCH = 128
OUT_CHUNKS_PER_ITER = 16
NEG = -1e30
LOG2E = 1.4426950408889634

BF16 = jnp.bfloat16
F32 = jnp.float32

VMEM_LIMIT = 56 * 1024 * 1024


def _dot(a, b):
    return jnp.dot(a, b, preferred_element_type=F32)


def _dot_nt(a, b):
    return lax.dot_general(a, b, (((1,), (1,)), ((), ())), preferred_element_type=F32)


def _rms(x, w):
    return x * lax.rsqrt(jnp.mean(x * x, axis=-1, keepdims=True) + EPS) * w


_WA, _G, _LF, _TOT, _ML = range(5)
N_GATE_Q = 5


def _cumsum_lanes(x):
    lane = lax.broadcasted_iota(jnp.int32, x.shape, 1)
    k = 1
    while k < x.shape[1]:
        x = x + jnp.where(lane >= k, pltpu.roll(x, k, axis=1), 0.0)
        k *= 2
    return x


def _in_proj_kernel(x_ref, xp_ref, xn_ref, nw_ref, wm_ref, wn_ref, wg_ref, gb_ref, cw_ref, cb_ref,
                    q_ref, k_ref, kt_ref, v_ref, o_ref, n_ref, g_ref, *, d_m, tm, steps_per_seq):
    i = pl.program_id(0)
    nsub = x_ref.shape[0] // tm
    dh = MLSTM_HD
    nw = nw_ref[...]
    c0 = 2 * d_m
    wb = 2 * dh
    row = lax.broadcasted_iota(jnp.int32, (tm, wb), 0)
    is_bwd = lax.broadcasted_iota(jnp.int32, (2 * MLSTM_HEADS, LCH), 0) >= MLSTM_HEADS
    wgt = wg_ref[...]
    seq_start = jnp.where(i % steps_per_seq == 0, 0.0, 1.0)
    seq_end = jnp.where(i % steps_per_seq == steps_per_seq - 1, 0.0, 1.0)

    for t in range(nsub):
        r0 = t * tm
        rs = slice(r0, r0 + tm)
        j0 = r0 // LCH
        x_prev = xp_ref[...] if t == 0 else x_ref[r0 - HALO:r0, :]
        x_next = xn_ref[...] if t == nsub - 1 else x_ref[r0 + tm:r0 + tm + HALO, :]
        first = seq_start if t == 0 else 1.0
        last = seq_end if t == nsub - 1 else 1.0
        hn = jnp.concatenate(
            [_rms(x_ref[rs, :], nw), _rms(x_prev, nw), _rms(x_next, nw)], axis=0).astype(BF16)

        def conv_act(pre, c0_):
            cs = slice(c0_, c0_ + wb)
            x = pre[0:tm]
            xm1 = jnp.where(row == 0, pre[tm + HALO - 1:tm + HALO] * first, pltpu.roll(x, 1, axis=0))
            xp1 = jnp.where(row == tm - 1, pre[tm + HALO:tm + HALO + 1] * last, pltpu.roll(x, tm - 1, axis=0))
            y = cw_ref[0:1, cs] * xm1 + cw_ref[1:2, cs] * x + cw_ref[2:3, cs] * xp1 + cb_ref[:, cs]
            return y * jax.nn.sigmoid(y)

        hn_t = hn[0:tm]
        pre_q0 = _dot_nt(hn, wm_ref[0:wb, :])

        gt = _dot_nt(wgt, hn_t)
        for j in range(tm // LCH):
            x = gt[:, j * LCH:(j + 1) * LCH] + gb_ref[...]
            x_f, x_b = x[0:2 * MLSTM_HEADS], x[2 * MLSTM_HEADS:]
            ig = jnp.where(is_bwd, pltpu.roll(x_b, MLSTM_HEADS, axis=0), x_f) * LOG2E
            f = jnp.where(is_bwd, x_b, pltpu.roll(x_f, MLSTM_HEADS, axis=0))
            lf = (jnp.minimum(f, 0.0) - jnp.log1p(jnp.exp(-jnp.abs(f)))) * LOG2E
            tot = jnp.sum(lf, axis=1, keepdims=True)
            csum = _cumsum_lanes(lf)
            b = jnp.where(is_bwd, tot - csum + lf, csum)
            a = tot - b + ig
            ml = jnp.max(a, axis=1, keepdims=True)
            g_ref[j0 + j, _WA] = jnp.exp2(a - ml)
            g_ref[j0 + j, _G] = ig - b
            g_ref[j0 + j, _LF] = lf
            g_ref[j0 + j, _TOT] = jnp.broadcast_to(tot, lf.shape)
            g_ref[j0 + j, _ML] = jnp.broadcast_to(ml, lf.shape)

        for hp in range(MLSTM_HEADS // 2):
            cs = slice(hp * wb, (hp + 1) * wb)
            qa = conv_act(pre_q0 if hp == 0 else _dot_nt(hn, wm_ref[cs, :]), hp * wb)
            ka = conv_act(_dot_nt(hn, wm_ref[d_m + hp * wb:d_m + (hp + 1) * wb, :]), d_m + hp * wb) * (dh ** -0.5)
            kb = ka.astype(BF16)
            v = _dot_nt(hn_t, wm_ref[c0 + hp * wb:c0 + (hp + 1) * wb, :]).astype(BF16)
            for hh in range(2):
                h = 2 * hp + hh
                hs = slice(hh * dh, (hh + 1) * dh)
                q_ref[h, rs, :] = qa[:, hs]
                k_ref[h, rs, :] = kb[:, hs]
                v_ref[h, rs, :] = v[:, hs]
                for j in range(tm // LCH):
                    kt_ref[h, j0 + j] = ka[j * LCH:(j + 1) * LCH, hs].T
        o_ref[rs, :] = _dot_nt(hn_t, wm_ref[c0 + d_m:c0 + 2 * d_m, :]).astype(BF16)
        n_ref[rs, :] = _dot_nt(hn_t, wn_ref[...]).astype(BF16)


HALO = 8
SUBTILES = 2


def _in_proj(x2, nw, wm, wn, wg, gb, cw, cb, *, layer, seq, d_m, tm):
    n, d = x2.shape
    d_n3 = wn.shape[1]
    nh, dh = MLSTM_HEADS, MLSTM_HD
    tb = SUBTILES * tm
    assert seq % tb == 0 and tm % LCH == 0
    kern = functools.partial(_in_proj_kernel, d_m=d_m, tm=tm, steps_per_seq=seq // tb)
    layer_blk = lambda a: pl.BlockSpec((None,) + a.shape[1:], lambda i: (layer, 0, 0),
                                       pipeline_mode=pl.Buffered(1))
    hb = tb // HALO
    return pl.pallas_call(
        kern,
        grid=(n // tb,),
        in_specs=[
            pl.BlockSpec((tb, d), lambda i: (i, 0)),
            pl.BlockSpec((HALO, d), lambda i: (jnp.maximum(i * hb - 1, 0), 0)),
            pl.BlockSpec((HALO, d), lambda i: (jnp.minimum((i + 1) * hb, n // HALO - 1), 0)),
            layer_blk(nw), layer_blk(wm), layer_blk(wn), layer_blk(wg), layer_blk(gb),
            layer_blk(cw), layer_blk(cb),
        ],
        out_specs=[
            pl.BlockSpec((nh, tb, dh), lambda i: (0, i, 0)),
            pl.BlockSpec((nh, tb, dh), lambda i: (0, i, 0)),
            pl.BlockSpec((nh, tb // LCH, dh, LCH), lambda i: (0, i, 0, 0)),
            pl.BlockSpec((nh, tb, dh), lambda i: (0, i, 0)),
            pl.BlockSpec((tb, d_m), lambda i: (i, 0)),
            pl.BlockSpec((tb, d_n3), lambda i: (i, 0)),
            pl.BlockSpec((tb // LCH, N_GATE_Q, 2 * nh, LCH), lambda i: (i, 0, 0, 0)),
        ],
        out_shape=[
            jax.ShapeDtypeStruct((nh, n, dh), F32),
            jax.ShapeDtypeStruct((nh, n, dh), BF16),
            jax.ShapeDtypeStruct((nh, n // LCH, dh, LCH), F32),
            jax.ShapeDtypeStruct((nh, n, dh), BF16),
            jax.ShapeDtypeStruct((n, d_m), BF16),
            jax.ShapeDtypeStruct((n, d_n3), BF16),
            jax.ShapeDtypeStruct((n // LCH, N_GATE_Q, 2 * nh, LCH), F32),
        ],
        compiler_params=pltpu.CompilerParams(
            dimension_semantics=("arbitrary",), vmem_limit_bytes=VMEM_LIMIT),
        name="in_proj",
    )(x2, x2, x2, nw, wm, wn, wg, gb, cw, cb)


def _mlstm_kernel(q_ref, k_ref, kt_ref, v_ref, g_ref, h_ref, sf, sb, stf, stb, mp, *, nch):
    hd = pl.program_id(1)
    dh = MLSTM_HD
    gate_row = (hd, MLSTM_HEADS + hd)
    ones = jnp.ones((LCH, dh), BF16)

    def v_aug(r0):
        return jnp.concatenate([v_ref[0, 0, pl.ds(r0, LCH), :], ones], axis=1)

    def grow(q_i, d, c):
        return g_ref[0, c, q_i, pl.ds(gate_row[d], 1), :]

    stf[...] = jnp.zeros_like(stf)
    stb[...] = jnp.zeros_like(stb)

    def scan_dir(c, m_prev, st, s_out, d):
        r0 = pl.multiple_of(c * LCH, LCH)
        tot = grow(_TOT, d, c)
        ml = grow(_ML, d, c)
        mp[d, pl.ds(c, 1), :] = m_prev
        s_prev = st[...]
        s_out[c] = s_prev.astype(BF16)
        a = (kt_ref[0, 0, c] * grow(_WA, d, c)).astype(BF16)
        u = _dot(a, v_aug(r0))
        m_new = jnp.maximum(tot + m_prev, ml)
        s_old = jnp.exp2(tot + m_prev - m_new)
        s_loc = jnp.exp2(ml - m_new)
        s_old2 = jnp.concatenate([s_old, s_old], axis=1)
        s_loc2 = jnp.concatenate([s_loc, s_loc], axis=1)
        st[...] = s_old2 * s_prev + s_loc2 * u
        return m_new

    def scan_body(i, carry):
        m_f, m_b = carry
        m_f = scan_dir(i, m_f, stf, sf, 0)
        m_b = scan_dir(nch - 1 - i, m_b, stb, sb, 1)
        return m_f, m_b

    zero_row = jnp.zeros((1, LCH), F32)
    lax.fori_loop(0, nch, scan_body, (zero_row, zero_row), unroll=8)

    jj = lax.broadcasted_iota(jnp.int32, (LCH, LCH), 0)
    ss = lax.broadcasted_iota(jnp.int32, (LCH, LCH), 1)
    mask_f = ss <= jj
    mask_b = ss >= jj

    dirs = ((mask_f, sf), (mask_b, sb))

    def weights_dir(c, q32, qk, mask, d):
        g_row = grow(_G, d, c)
        lf_row = grow(_LF, d, c)
        m_prev = mp[d, pl.ds(c, 1), :]
        gm = jnp.where(mask, g_row, -jnp.inf)
        mj = jnp.maximum(jnp.max(gm, axis=1, keepdims=True), m_prev)
        bj = jnp.sum(jnp.where(mask, lf_row, 0.0), axis=1, keepdims=True)
        p = (jnp.exp2(gm - mj) * qk).astype(BF16)
        wq = (jnp.exp2(m_prev - mj) * q32).astype(BF16)
        return jnp.concatenate([wq, p], axis=1), jnp.exp2(-bj - mj)

    def out_body(it, carry):
        cs = [it * OUT_CHUNKS_PER_ITER + j for j in range(OUT_CHUNKS_PER_ITER)]
        r0s = [pl.multiple_of(c * LCH, LCH) for c in cs]
        q32 = [q_ref[0, 0, pl.ds(r0, LCH), :] for r0 in r0s]
        qk = [_dot_nt(q.astype(BF16), k_ref[0, 0, pl.ds(r0, LCH), :]) for q, r0 in zip(q32, r0s)]
        wts = [[weights_dir(c, q, s, mask, d) for d, (mask, _) in enumerate(dirs)]
               for c, q, s in zip(cs, q32, qk)]
        for c, r0, wt in zip(cs, r0s, wts):
            vc = v_aug(r0)
            h = None
            for (lhs, bound), (_, s_ref) in zip(wt, dirs):
                tot = _dot(lhs, jnp.concatenate([s_ref[c], vc], axis=0))
                hd_ = tot[:, 0:dh] / jnp.maximum(jnp.abs(tot[:, dh:2 * dh]), bound)
                h = hd_ if h is None else h + hd_
            h_ref[0, 0, pl.ds(r0, LCH), :] = h.astype(BF16)
        return carry

    lax.fori_loop(0, nch // OUT_CHUNKS_PER_ITER, out_body, 0)


def _mlstm(q, k, kt, v, gates):
    nh, bsz, seq, dh = q.shape
    nch = seq // LCH
    kern = functools.partial(_mlstm_kernel, nch=nch)
    seq_blk = pl.BlockSpec((1, 1, seq, dh), lambda b, h: (h, b, 0, 0))
    return pl.pallas_call(
        kern,
        grid=(bsz, nh),
        in_specs=[
            seq_blk, seq_blk,
            pl.BlockSpec((1, 1, nch, dh, LCH), lambda b, h: (h, b, 0, 0, 0)),
            seq_blk,
            pl.BlockSpec((1, nch, N_GATE_Q, 2 * nh, LCH), lambda b, h: (b, 0, 0, 0, 0)),
        ],
        out_specs=seq_blk,
        out_shape=jax.ShapeDtypeStruct((nh, bsz, seq, dh), BF16),
        scratch_shapes=[
            pltpu.VMEM((nch, dh, 2 * dh), BF16),
            pltpu.VMEM((nch, dh, 2 * dh), BF16),
            pltpu.VMEM((dh, 2 * dh), F32),
            pltpu.VMEM((dh, 2 * dh), F32),
            pltpu.VMEM((2, nch, LCH), F32),
        ],
        compiler_params=pltpu.CompilerParams(
            dimension_semantics=("arbitrary", "arbitrary"), vmem_limit_bytes=VMEM_LIMIT),
        name="mlstm",
    )(q, k, kt, v, gates)


NA_ROWS_PER_STEP = 16
NA_ROWS_PER_ITER = 4


def _natten_kernel(q_ref, k_ref, v_ref, rpb_ref, o_ref, bias_ref, s_scr, e_scr, l_scr, *, rows):
    rb = pl.program_id(1)
    band = NA_KH * GRID_W
    npair = NA_HEADS // 2
    lane = lax.broadcasted_iota(jnp.int32, (GRID_W, 2 * NA_HD), 1)
    lo = lane < NA_HD
    units = [(j, p) for j in range(NA_ROWS_PER_ITER) for p in range(npair)]

    @pl.when((pl.program_id(0) == 0) & (rb == 0))
    def _():
        n_rr = 2 * NA_KH - 1
        col = lax.broadcasted_iota(jnp.int32, (GRID_W, 2 * GRID_W), 0)
        kcl = lax.broadcasted_iota(jnp.int32, (GRID_W, 2 * GRID_W), 1)
        c_start = jnp.clip(col - NA_KW // 2, 0, GRID_W - NA_KW)
        valid = (kcl >= c_start) & (kcl < c_start + NA_KW)
        for h in range(NA_HEADS):
            toep = [jnp.where(valid,
                              pltpu.roll(jnp.broadcast_to(rpb_ref[h, rr:rr + 1, :], col.shape),
                                         2 * GRID_W - (NA_KW - 1), 1, stride=1, stride_axis=0),
                              NEG)
                    for rr in range(n_rr)]
            pair = [jnp.where(kcl < GRID_W, toep[rr], pltpu.roll(toep[rr + 1], GRID_W, 1))
                    for rr in range(n_rr - 1)]
            for d in range(NA_KH):
                for m in range(NA_KH // 2):
                    bias_ref[h // 2, d, (h % 2) * GRID_W:(h % 2 + 1) * GRID_W,
                             2 * m * GRID_W:2 * (m + 1) * GRID_W] = pair[NA_KH - 1 - d + 2 * m]

    def rows_body(it, carry):
        q0s, k0s, ds = [], [], []
        for j in range(NA_ROWS_PER_ITER):
            i = it * NA_ROWS_PER_ITER + j
            r = rb * NA_ROWS_PER_STEP + i
            rs = jnp.clip(r - NA_KH // 2, 0, rows - NA_KH)
            ds.append(r - rs)
            q0s.append(pl.multiple_of(i * GRID_W, GRID_W))
            k0s.append(pl.multiple_of(rs * GRID_W, GRID_W))
        for u, (j, p) in enumerate(units):
            cs = slice(p * 2 * NA_HD, (p + 1) * 2 * NA_HD)
            qp = q_ref[0, pl.ds(q0s[j], GRID_W), cs].astype(F32)
            qq = jnp.concatenate([jnp.where(lo, qp, 0.0), jnp.where(lo, 0.0, qp)], axis=0).astype(BF16)
            s_scr[u] = _dot_nt(qq, k_ref[0, pl.ds(k0s[j], band), cs]) + bias_ref[p, ds[j]]
        for u in range(len(units)):
            s = s_scr[u]
            e = jnp.exp2(s - jnp.max(s, axis=-1, keepdims=True))
            l_scr[u] = jnp.broadcast_to(jnp.sum(e, axis=-1, keepdims=True), l_scr.shape[1:])
            e_scr[u] = e.astype(BF16)
        for u, (j, p) in enumerate(units):
            cs = slice(p * 2 * NA_HD, (p + 1) * 2 * NA_HD)
            o2 = _dot(e_scr[u], v_ref[0, pl.ds(k0s[j], band), cs]) / l_scr[u]
            op = jnp.where(lo, o2[0:GRID_W], o2[GRID_W:2 * GRID_W])
            o_ref[0, pl.ds(q0s[j], GRID_W), cs] = op.astype(BF16)
        return carry

    lax.fori_loop(0, NA_ROWS_PER_STEP // NA_ROWS_PER_ITER, rows_body, 0)


def _natten(qkv, rpb_pad, *, layer):
    bsz, seq, d3 = qkv.shape
    d_n = d3 // 3
    rows = seq // GRID_W
    tq = NA_ROWS_PER_STEP * GRID_W
    n_units = NA_ROWS_PER_ITER * NA_HEADS // 2
    kern = functools.partial(_natten_kernel, rows=rows)
    return pl.pallas_call(
        kern,
        grid=(bsz, rows // NA_ROWS_PER_STEP),
        in_specs=[
            pl.BlockSpec((1, tq, d_n), lambda b, r: (b, r, 0)),
            pl.BlockSpec((1, seq, d_n), lambda b, r: (b, 0, 1)),
            pl.BlockSpec((1, seq, d_n), lambda b, r: (b, 0, 2)),
            pl.BlockSpec((None,) + rpb_pad.shape[1:], lambda b, r: (layer, 0, 0, 0)),
        ],
        out_specs=pl.BlockSpec((1, tq, d_n), lambda b, r: (b, r, 0)),
        out_shape=jax.ShapeDtypeStruct((bsz, seq, d_n), BF16),
        scratch_shapes=[
            pltpu.VMEM((NA_HEADS // 2, NA_KH, 2 * GRID_W, NA_KH * GRID_W), F32),
            pltpu.VMEM((n_units, 2 * GRID_W, NA_KH * GRID_W), F32),
            pltpu.VMEM((n_units, 2 * GRID_W, NA_KH * GRID_W), BF16),
            pltpu.VMEM((n_units, 2 * GRID_W, 2 * NA_HD), F32),
        ],
        compiler_params=pltpu.CompilerParams(
            dimension_semantics=("arbitrary", "arbitrary"), vmem_limit_bytes=VMEM_LIMIT),
        name="natten",
    )(qkv, qkv, qkv, rpb_pad)


def _pad_rpb(rpb):
    n_rr, n_rc = rpb.shape[2], rpb.shape[3]
    return jnp.pad(rpb.astype(F32) * LOG2E, ((0, 0), (0, 0), (0, 2 * NA_KH - n_rr), (0, 2 * GRID_W - n_rc)))


FF_CHUNK = 1024


def _out_ffn_kernel(x_ref, hm_ref, om_ref, yn_ref, mw_ref, wo_ref, nw_ref, w1_ref, w2_ref, fw_ref,
                    o_ref, x1_ref, *, final, tm):
    nh = hm_ref.shape[0]
    d_m = nh * hm_ref.shape[2]
    d_ff = w1_ref.shape[1]
    for t in range(x_ref.shape[0] // tm):
        rs = slice(t * tm, (t + 1) * tm)
        hm = [hm_ref[h, rs, :].astype(F32) for h in range(nh)]
        hn = jnp.concatenate(
            [hh * lax.rsqrt(jnp.mean(jnp.square(hh), axis=-1, keepdims=True) + EPS) for hh in hm],
            axis=1) * mw_ref[...]
        ym = (jax.nn.sigmoid(om_ref[rs, :].astype(F32)) * hn).astype(BF16)
        x1_ref[rs, :] = x_ref[rs, :] + _dot(yn_ref[rs, :], wo_ref[d_m:, :]) + _dot(ym, wo_ref[0:d_m, :])
        h = _rms(x1_ref[rs, :], nw_ref[...]).astype(BF16)
        ffn = None
        for j in range(d_ff // FF_CHUNK):
            cs = slice(j * FF_CHUNK, (j + 1) * FF_CHUNK)
            hid = jnp.square(jnp.maximum(_dot(h, w1_ref[:, cs]), 0.0)).astype(BF16)
            part = _dot(hid, w2_ref[cs, :])
            ffn = part if ffn is None else ffn + part
        acc = x1_ref[rs, :] + ffn
        if final:
            acc = _rms(acc, fw_ref[...])
        o_ref[rs, :] = acc


def _out_ffn(x2, hm, om, yn, mw, wo, nw, w1, w2, fw, *, layer, final, tm):
    n, d = x2.shape
    d_m, d_n = om.shape[1], yn.shape[1]
    tb = SUBTILES * tm
    kern = functools.partial(_out_ffn_kernel, final=final, tm=tm)
    layer_blk = lambda a: pl.BlockSpec((None,) + a.shape[1:], lambda i: (layer, 0, 0),
                                       pipeline_mode=pl.Buffered(1))
    return pl.pallas_call(
        kern,
        grid=(n // tb,),
        in_specs=[
            pl.BlockSpec((tb, d), lambda i: (i, 0)),
            pl.BlockSpec((hm.shape[0], tb, hm.shape[2]), lambda i: (0, i, 0)),
            pl.BlockSpec((tb, d_m), lambda i: (i, 0)),
            pl.BlockSpec((tb, d_n), lambda i: (i, 0)),
            layer_blk(mw), layer_blk(wo), layer_blk(nw), layer_blk(w1), layer_blk(w2),
            pl.BlockSpec(memory_space=pltpu.VMEM),
        ],
        out_specs=pl.BlockSpec((tb, d), lambda i: (i, 0)),
        out_shape=jax.ShapeDtypeStruct((n, d), F32),
        scratch_shapes=[pltpu.VMEM((tb, d), F32)],
        compiler_params=pltpu.CompilerParams(
            dimension_semantics=("arbitrary",), vmem_limit_bytes=VMEM_LIMIT),
        name="out_ffn",
    )(x2, hm, om, yn, mw, wo, nw, w1, w2, fw)


def kernel(x, norm1_w, w_in, conv_w, conv_b, gate_b, mlstm_norm_w, rpb, w_out, norm2_w, w_ff1, w_ff2,
           final_norm_w):
    bsz, seq, d = x.shape
    depth = w_in.shape[0]
    d_m = MLSTM_HEADS * MLSTM_HD
    d_n = NA_HEADS * NA_HD
    n = bsz * seq
    rows = seq // GRID_W
    nch = seq // LCH
    assert w_in.shape[2] == 4 * d_m + N_GATES + 3 * d_n
    assert seq % LCH == 0 and rows % NA_ROWS_PER_STEP == 0 and rows >= NA_KH

    g0 = 4 * d_m
    n0 = g0 + N_GATES
    w_t = jnp.swapaxes(w_in, 1, 2)
    w_m = w_t[:, 0:g0].astype(BF16)
    w_g = w_t[:, g0:n0].astype(BF16)
    w_n = jnp.concatenate([w_t[:, n0:n0 + d_n] * (NA_HD ** -0.5 * LOG2E),
                           w_t[:, n0 + d_n:]], axis=1).astype(BF16)
    gb = jnp.broadcast_to(gate_b[:, :, None], (depth, N_GATES, LCH))
    w_o, w_1, w_2 = w_out.astype(BF16), w_ff1.astype(BF16), w_ff2.astype(BF16)
    n1, n2 = norm1_w[:, None, :], norm2_w[:, None, :]
    cb, mn = conv_b[:, None, :], mlstm_norm_w[:, None, :]
    rpb_pad = _pad_rpb(rpb)

    x2 = x.reshape(n, d)
    for l in range(depth):
        q, k, kt, v, o, qkv_n, gates = _in_proj(
            x2, n1, w_m, w_n, w_g, gb, conv_w, cb, layer=l, seq=seq, d_m=d_m, tm=512)
        hm_shape = (MLSTM_HEADS, bsz, seq, MLSTM_HD)
        h_m = _mlstm(
            q.reshape(hm_shape), k.reshape(hm_shape), kt.reshape(MLSTM_HEADS, bsz, nch, MLSTM_HD, LCH),
            v.reshape(hm_shape), gates.reshape(bsz, nch, N_GATE_Q, 2 * MLSTM_HEADS, LCH))
        y_n = _natten(qkv_n.reshape(bsz, seq, 3 * d_n), rpb_pad, layer=l)
        x2 = _out_ffn(
            x2, h_m.reshape(MLSTM_HEADS, n, MLSTM_HD), o, y_n.reshape(n, d_n), mn, w_o, n2, w_1, w_2,
            final_norm_w[None], layer=l, final=(l == depth - 1), tm=512)
    return x2.reshape(bsz, seq, d)
```
